```python
import jax, jax.numpy as jnp
from jax import lax
import numpy as np

D_MODEL = 2048
BATCH = 4
SEQ = 8192
DEPTH = 1
DEC_BATCH = 2
DEC_SEQ = 16384
PAST_LEN = 128

HEAD_DIM = 128
A_Q_HEADS = 8
A_KV_HEADS = 2
A_GROUP = A_Q_HEADS // A_KV_HEADS
B_PAIRS = ((128, 1), (512, 4), (2048, 16))
B_N_GROUPS = len(B_PAIRS)
B_HEADS_PER_GROUP = 4
GRID_W = 64
AXIAL_THETA = 10000.0
ROPE_THETA = 500000.0
ROPE_DIM = HEAD_DIM // 4
Q_BLOCK = 128
NORM_EPS = 1e-6
NEG_INF = -1e30
ATTN_SCALE = HEAD_DIM ** -0.5

A_Q_W = A_Q_HEADS * HEAD_DIM
A_KV_W = A_KV_HEADS * HEAD_DIM
B_W = B_N_GROUPS * B_HEADS_PER_GROUP * HEAD_DIM
B_OUT_W = B_HEADS_PER_GROUP * HEAD_DIM
IN_WIDTH = A_Q_W + 2 * A_KV_W + 3 * B_W + 2 * D_MODEL
SPLITS = tuple(int(v) for v in np.cumsum([A_Q_W, A_KV_W, A_KV_W, B_W, B_W, B_W, D_MODEL]))

N_EXPERT_GROUPS = 4
EXPERTS_PER_GROUP = 8
N_EXPERTS = N_EXPERT_GROUPS * EXPERTS_PER_GROUP
TOP_K = 2
D_EXPERT = 512
ROUTE_BLOCK = 128

kernel_name = 'hybrid_axial_gqa_dilated_hmoe_encoder'


def rmsnorm(x, g):
    xf = x.astype(jnp.float32)
    y = xf * lax.rsqrt(jnp.mean(xf * xf, axis=-1, keepdims=True) + NORM_EPS)
    return (y * g.astype(jnp.float32)).astype(x.dtype)


def rope(x, pos, theta):
    half = x.shape[-1] // 2
    inv = jnp.power(jnp.float32(theta), -jnp.arange(half, dtype=jnp.float32) / half)
    ang = pos.astype(jnp.float32)[:, None] * inv[None, :]
    cos = jnp.cos(ang)[None, :, None, :]
    sin = jnp.sin(ang)[None, :, None, :]
    xf = x.astype(jnp.float32)
    x1, x2 = xf[..., :half], xf[..., half:]
    return jnp.concatenate([x1 * cos - x2 * sin, x2 * cos + x1 * sin], axis=-1).astype(x.dtype)


def axial_rope(x, seq_len):
    rows = seq_len // GRID_W
    r_idx = jnp.repeat(jnp.arange(rows), GRID_W)
    c_idx = jnp.tile(jnp.arange(GRID_W), rows)
    half = x.shape[-1] // 2
    return jnp.concatenate([rope(x[..., :half], r_idx, AXIAL_THETA),
                            rope(x[..., half:], c_idx, AXIAL_THETA)], axis=-1)


def partial_rope(x, seq_len):
    pos = jnp.arange(seq_len)
    return jnp.concatenate([rope(x[..., :ROPE_DIM], pos, ROPE_THETA), x[..., ROPE_DIM:]], axis=-1)


def gqa_attention(q, k, v):
    bsz, s = q.shape[0], q.shape[1]
    nb = s // Q_BLOCK
    qb = q.reshape(bsz, nb, Q_BLOCK, A_KV_HEADS, A_GROUP, HEAD_DIM).transpose(1, 0, 2, 3, 4, 5)

    def block(qi):
        sc = jnp.einsum('bqgrd,bkgd->bgrqk', qi, k, preferred_element_type=jnp.float32) * ATTN_SCALE
        p = jax.nn.softmax(sc, axis=-1)
        return jnp.einsum('bgrqk,bkgd->bqgrd', p.astype(v.dtype), v)

    o = lax.map(block, qb)
    return o.transpose(1, 0, 2, 3, 4, 5).reshape(bsz, s, A_Q_W)


def banded_dilated(q, k, v, window, dilation):
    bsz, s, h, d = q.shape
    L = s // dilation
    n = window // (2 * dilation)
    nb = -(-L // n)
    lp = nb * n
    N = bsz * dilation

    def fold(t):
        return t.reshape(bsz, L, dilation, h, d).transpose(0, 2, 1, 3, 4).reshape(N, L, h, d)

    qf, kf, vf = fold(q), fold(k), fold(v)
    qp = jnp.pad(qf, ((0, 0), (0, lp - L), (0, 0), (0, 0))).reshape(N, nb, n, h, d)

    def windows(t):
        tp = jnp.pad(t, ((0, 0), (n, lp - L + n), (0, 0), (0, 0))).reshape(N, nb + 2, n, h, d)
        return jnp.concatenate([tp[:, :-2], tp[:, 1:-1], tp[:, 2:]], axis=2)

    kw, vw = windows(kf), windows(vf)
    sc = jnp.einsum('nbqhd,nbkhd->nbhqk', qp, kw, preferred_element_type=jnp.float32) * ATTN_SCALE
    i = jnp.arange(n)[:, None]
    j = jnp.arange(3 * n)[None, :]
    band = jnp.abs(j - n - i) <= n
    key_u = (jnp.arange(nb)[:, None] - 1) * n + jnp.arange(3 * n)[None, :]
    kvalid = (key_u >= 0) & (key_u < L)
    mask = band[None, :, :] & kvalid[:, None, :]
    sc = jnp.where(mask[None, :, None, :, :], sc, NEG_INF)
    m = jnp.max(sc, axis=-1, keepdims=True)
    p = jnp.exp(sc - m)
    den = jnp.sum(p, axis=-1)
    o = jnp.einsum('nbhqk,nbkhd->nbqhd', p.astype(v.dtype), vw).astype(jnp.float32)
    den_q = den.transpose(0, 1, 3, 2)
    o = o / den_q[..., None]
    lse = (m[..., 0] + jnp.log(den)).transpose(0, 1, 3, 2)

    def unfold(t):
        rest = t.shape[3:]
        t = t.reshape((N, lp) + rest)[:, :L]
        t = jnp.swapaxes(t.reshape((bsz, dilation, L) + rest), 1, 2)
        return t.reshape((bsz, s) + rest)

    return unfold(o), unfold(lse)


def dilated_mixture(qb, kb, vb):
    bsz, s = qb.shape[0], qb.shape[1]
    outs, lses = [], []
    for g, (w, r) in enumerate(B_PAIRS):
        o, l = banded_dilated(qb[:, :, g], kb[:, :, g], vb[:, :, g], w, r)
        outs.append(o)
        lses.append(l)
    alpha = jax.nn.softmax(jnp.stack(lses, axis=0), axis=0)
    o = jnp.sum(alpha[..., None] * jnp.stack(outs, axis=0), axis=0)
    return o.reshape(bsz, s, B_OUT_W).astype(qb.dtype)


def hier_moe(h, w_rg, b_rg, w_re, b_re, w1, w3, w2):
    T = h.shape[0]
    pg = jax.nn.softmax((h @ w_rg).astype(jnp.float32) + b_rg, axis=-1)
    g_sel = jnp.argmax(pg, axis=-1)
    g_w = jnp.take_along_axis(pg, g_sel[:, None], axis=-1)[:, 0]
    le = ((h @ w_re).astype(jnp.float32) + b_re).reshape(T, N_EXPERT_GROUPS, EXPERTS_PER_GROUP)
    le_sel = jnp.take_along_axis(le, g_sel[:, None, None], axis=1)[:, 0]
    top_p, top_i = lax.top_k(jax.nn.softmax(le_sel, axis=-1), TOP_K)
    top_p = top_p / jnp.sum(top_p, axis=-1, keepdims=True)
    wts = (g_w[:, None] * top_p).reshape(-1)
    e_id = (g_sel[:, None] * EXPERTS_PER_GROUP + top_i).reshape(-1).astype(jnp.int32)
    tok = jnp.repeat(jnp.arange(T, dtype=jnp.int32), TOP_K)
    A = T * TOP_K
    order = jnp.argsort(e_id)
    e_s, tok_s, w_s = e_id[order], tok[order], wts[order]
    counts = jnp.bincount(e_id, length=N_EXPERTS).astype(jnp.int32)
    starts = jnp.cumsum(counts) - counts
    pcounts = (counts + ROUTE_BLOCK - 1) // ROUTE_BLOCK * ROUTE_BLOCK
    pends = jnp.cumsum(pcounts)
    pstarts = pends - pcounts
    dest = pstarts[e_s] + (jnp.arange(A, dtype=jnp.int32) - starts[e_s])
    P = A + N_EXPERTS * ROUTE_BLOCK
    nblk = P // ROUTE_BLOCK
    row_tok = jnp.full((P,), T, jnp.int32).at[dest].set(tok_s)
    row_w = jnp.zeros((P,), jnp.float32).at[dest].set(w_s)
    blk_e = jnp.minimum(jnp.searchsorted(pends, jnp.arange(nblk, dtype=jnp.int32) * ROUTE_BLOCK,
                                         side='right'), N_EXPERTS - 1)
    hp = jnp.concatenate([h, jnp.zeros((1, h.shape[1]), h.dtype)], axis=0)
    xin = hp[row_tok].reshape(nblk, ROUTE_BLOCK, h.shape[1])

    def expert_block(args):
        xb, e = args
        return (jax.nn.silu(xb @ w1[e]) * (xb @ w3[e])) @ w2[e]

    out = lax.map(expert_block, (xin, blk_e)).reshape(P, h.shape[1])
    out = out * row_w[:, None].astype(out.dtype)
    return jnp.zeros((T + 1, h.shape[1]), out.dtype).at[row_tok].add(out)[:T]


def layer(x, g_mix, w_in, g_qa, g_ka, g_qb, g_kb, w_oa, w_ob, w_out,
          g_ffn, w_rg, b_rg, w_re, b_re, w1, w3, w2):
    bsz, s, _ = x.shape
    h = rmsnorm(x, g_mix)
    proj = h @ w_in
    qa, ka, va, qb, kb, vb, ga, gb = jnp.split(proj, SPLITS, axis=-1)
    qa = axial_rope(rmsnorm(qa.reshape(bsz, s, A_Q_HEADS, HEAD_DIM), g_qa), s)
    ka = axial_rope(rmsnorm(ka.reshape(bsz, s, A_KV_HEADS, HEAD_DIM), g_ka), s)
    va = va.reshape(bsz, s, A_KV_HEADS, HEAD_DIM)
    ya = gqa_attention(qa, ka, va) @ w_oa
    nbh = B_N_GROUPS * B_HEADS_PER_GROUP
    bshape = (bsz, s, B_N_GROUPS, B_HEADS_PER_GROUP, HEAD_DIM)
    qb = partial_rope(rmsnorm(qb.reshape(bsz, s, nbh, HEAD_DIM), g_qb), s).reshape(bshape)
    kb = partial_rope(rmsnorm(kb.reshape(bsz, s, nbh, HEAD_DIM), g_kb), s).reshape(bshape)
    vb = vb.reshape(bshape)
    yb = dilated_mixture(qb, kb, vb) @ w_ob
    mixed = jax.nn.sigmoid(ga) * ya + jax.nn.sigmoid(gb) * yb
    x = x + mixed @ w_out
    h2 = rmsnorm(x, g_ffn).reshape(bsz * s, D_MODEL)
    return x + hier_moe(h2, w_rg, b_rg, w_re, b_re, w1, w3, w2).reshape(bsz, s, D_MODEL)


def setup_inputs(seed: int = 0) -> dict:
    key = jax.random.key(seed)
    ks = jax.random.split(key, 19)
    f32 = jnp.float32

    def nrm(k, shape, scale):
        return jax.random.normal(k, shape, f32) * scale

    def gain(k, shape):
        return 1.0 + 0.02 * jax.random.normal(k, shape, f32)

    return {
        'x_prompt': nrm(ks[0], (BATCH, SEQ, D_MODEL), 1.0),
        'x_sample': nrm(ks[1], (DEC_BATCH, DEC_SEQ, D_MODEL), 1.0),
        'g_mix': gain(ks[2], (DEPTH, D_MODEL)),
        'w_in': nrm(ks[3], (DEPTH, D_MODEL, IN_WIDTH), D_MODEL ** -0.5),
        'g_qa': gain(ks[4], (DEPTH, HEAD_DIM)),
        'g_ka': gain(ks[5], (DEPTH, HEAD_DIM)),
        'g_qb': gain(ks[6], (DEPTH, HEAD_DIM)),
        'g_kb': gain(ks[7], (DEPTH, HEAD_DIM)),
        'w_oa': nrm(ks[8], (DEPTH, A_Q_W, D_MODEL), A_Q_W ** -0.5),
        'w_ob': nrm(ks[9], (DEPTH, B_OUT_W, D_MODEL), B_OUT_W ** -0.5),
        'w_out': nrm(ks[10], (DEPTH, D_MODEL, D_MODEL), D_MODEL ** -0.5),
        'g_ffn': gain(ks[11], (DEPTH, D_MODEL)),
        'w_rg': nrm(ks[12], (DEPTH, D_MODEL, N_EXPERT_GROUPS), D_MODEL ** -0.5),
        'b_rg': nrm(ks[13], (DEPTH, N_EXPERT_GROUPS), 0.01),
        'w_re': nrm(ks[14], (DEPTH, D_MODEL, N_EXPERTS), D_MODEL ** -0.5),
        'b_re': nrm(ks[15], (DEPTH, N_EXPERTS), 0.01),
        'w1': nrm(ks[16], (DEPTH, N_EXPERTS, D_MODEL, D_EXPERT), D_MODEL ** -0.5),
        'w3': nrm(ks[17], (DEPTH, N_EXPERTS, D_MODEL, D_EXPERT), D_MODEL ** -0.5),
        'w2': nrm(ks[18], (DEPTH, N_EXPERTS, D_EXPERT, D_MODEL), D_EXPERT ** -0.5),
    }


def reference(x_prompt, x_sample, g_mix, w_in, g_qa, g_ka, g_qb, g_kb, w_oa, w_ob, w_out,
              g_ffn, w_rg, b_rg, w_re, b_re, w1, w3, w2):
    y_prompt = x_prompt
    y_sample = x_sample
    for l in range(DEPTH):
        args = (g_mix[l], w_in[l], g_qa[l], g_ka[l], g_qb[l], g_kb[l], w_oa[l], w_ob[l], w_out[l],
                g_ffn[l], w_rg[l], b_rg[l], w_re[l], b_re[l], w1[l], w3[l], w2[l])
        y_prompt = layer(y_prompt, *args)
        y_sample = layer(y_sample, *args)
    return (y_prompt, y_sample)
```

```python
import functools
import math

import jax
import jax.numpy as jnp
import numpy as np
from jax import lax
from jax.experimental import pallas as pl
from jax.experimental.pallas import tpu as pltpu

F32 = jnp.float32
BF16 = jnp.bfloat16
I32 = jnp.int32

D_MODEL = 2048
HEAD_DIM = 128
A_Q_HEADS = 8
A_KV_HEADS = 2
A_GROUP = A_Q_HEADS // A_KV_HEADS
B_PAIRS = ((128, 1), (512, 4), (2048, 16))
B_HEADS_PER_GROUP = 4
B_HALF_WINDOW = 64
GRID_W = 64
AXIAL_THETA = 10000.0
ROPE_THETA = 500000.0
ROPE_DIM = HEAD_DIM // 4
NORM_EPS = 1e-6
NEG_INF = -1e30
ATTN_SCALE = HEAD_DIM ** -0.5
LOG2E = 1.4426950408889634

A_Q_W = A_Q_HEADS * HEAD_DIM
A_KV_W = A_KV_HEADS * HEAD_DIM
B_W = len(B_PAIRS) * B_HEADS_PER_GROUP * HEAD_DIM
B_OUT_W = B_HEADS_PER_GROUP * HEAD_DIM
IN_WIDTH = A_Q_W + 2 * A_KV_W + 3 * B_W + 2 * D_MODEL
COL_QA = 0
COL_KA = A_Q_W
COL_VA = COL_KA + A_KV_W
COL_QB = COL_VA + A_KV_W
COL_KB = COL_QB + B_W
COL_VB = COL_KB + B_W
COL_GA = COL_VB + B_W
COL_GB = COL_GA + D_MODEL

N_EXPERT_GROUPS = 4
EXPERTS_PER_GROUP = 8
N_EXPERTS = N_EXPERT_GROUPS * EXPERTS_PER_GROUP
TOP_K = 2
D_EXPERT = 512
ROUTER_ROWS = 128
ROUTER_EXPERT_ROW0 = 8

LANE = 128
VMEM_LIMIT = 56 * 1024 * 1024
TM_PROJ = 512
TN_PROJ = 1024
TQ_A = 512
TK_A = 512
BQ_B = 256
TM_MIX = 512
TN_MIX = 1024
TM_OUT = 512
TN_OUT = 1024
MOE_BLK = 512
TM_COMB = 256


def _cparams(sem):
    return pltpu.CompilerParams(dimension_semantics=sem, vmem_limit_bytes=VMEM_LIMIT)


def _chunk_classes():
    cls = []
    for c in range(IN_WIDTH // LANE):
        col = c * LANE
        if col < COL_KA:
            cls.append("qa")
        elif col < COL_VA:
            cls.append("ka")
        elif col < COL_QB:
            cls.append("plain")
        elif col < COL_KB:
            cls.append("qb")
        elif col < COL_VB:
            cls.append("kb")
        else:
            cls.append("plain")
    return cls


def _rope_apply(zn, tab_ref, half):
    c = tab_ref[:, 0:LANE]
    sp = tab_ref[:, LANE:2 * LANE]
    sm = tab_ref[:, 2 * LANE:3 * LANE]
    return zn * c + pltpu.roll(zn, half, 1) * sp + pltpu.roll(zn, LANE - half, 1) * sm


def _inproj_kernel(xp_ref, xs_ref, gmix_ref, w_ref, taba_ref, tabb_ref, gains_ref, o_ref, h_ref,
                   *, n_p_tiles, patterns):
    i = pl.program_id(0)
    j = pl.program_id(1)

    def norm_from(x_ref):
        x = x_ref[...]
        ms = jnp.mean(x * x, axis=-1, keepdims=True)
        h_ref[...] = (x * lax.rsqrt(ms + NORM_EPS) * gmix_ref[...]).astype(BF16)

    @pl.when(jnp.logical_and(j == 0, i < n_p_tiles))
    def _():
        norm_from(xp_ref)

    @pl.when(jnp.logical_and(j == 0, i >= n_p_tiles))
    def _():
        norm_from(xs_ref)

    z = jnp.dot(h_ref[...], w_ref[...], preferred_element_type=F32)

    def qk_chunk(zc, gain_row, tab_ref, half):
        ms = jnp.mean(zc * zc, axis=-1, keepdims=True)
        zn = zc * lax.rsqrt(ms + NORM_EPS) * gains_ref[gain_row:gain_row + 1, :]
        return _rope_apply(zn, tab_ref, half)

    for tiles, classes in patterns:
        cond = functools.reduce(jnp.logical_or, [j == t for t in tiles])

        @pl.when(cond)
        def _(classes=classes):
            for c, cl in enumerate(classes):
                zc = z[:, c * LANE:(c + 1) * LANE]
                if cl == "qa":
                    r = qk_chunk(zc, 0, taba_ref, 32)
                elif cl == "ka":
                    r = qk_chunk(zc, 1, taba_ref, 32)
                elif cl == "qb":
                    r = qk_chunk(zc, 2, tabb_ref, 16)
                elif cl == "kb":
                    r = qk_chunk(zc, 3, tabb_ref, 16)
                else:
                    r = zc
                o_ref[:, c * LANE:(c + 1) * LANE] = r.astype(BF16)


def _rope_tables(smax):
    t = jnp.arange(smax, dtype=I32)

    def table(pos_list, half, width):
        cos = jnp.ones((smax, width), F32)
        sp = jnp.zeros((smax, width), F32)
        sm = jnp.zeros((smax, width), F32)
        inv = jnp.power(jnp.float32(pos_list[0][2]), -jnp.arange(half, dtype=F32) / half)
        for pos, start, _ in pos_list:
            ang = pos.astype(F32)[:, None] * inv[None, :]
            cs, sn = jnp.cos(ang), jnp.sin(ang)
            cos = cos.at[:, start:start + half].set(cs).at[:, start + half:start + 2 * half].set(cs)
            sm = sm.at[:, start:start + half].set(-sn)
            sp = sp.at[:, start + half:start + 2 * half].set(sn)
        return jnp.concatenate([cos, sp, sm], axis=1)

    tab_a = table([(t // GRID_W, 0, AXIAL_THETA), (t % GRID_W, HEAD_DIM // 2, AXIAL_THETA)], HEAD_DIM // 4, LANE)
    tab_b = table([(t, 0, ROPE_THETA)], ROPE_DIM // 2, LANE)
    return tab_a, tab_b


def _in_proj(xp, xs, sp_len, ss_len, g_mix, w_in, g_qa, g_ka, g_qb, g_kb):
    tp, ts = xp.shape[0], xs.shape[0]
    tm, tn = TM_PROJ, TN_PROJ
    assert tp % tm == 0 and ts % tm == 0 and sp_len % tm == 0 and ss_len % tm == 0
    n_p, n_s = tp // tm, ts // tm
    nj = IN_WIDTH // tn
    cls = _chunk_classes()
    per_tile = tn // LANE
    groups = {}
    for jt in range(nj):
        groups.setdefault(tuple(cls[jt * per_tile:(jt + 1) * per_tile]), []).append(jt)
    patterns = tuple((tuple(v), k) for k, v in groups.items())

    smax = max(sp_len, ss_len)
    tab_a, tab_b = _rope_tables(smax)
    gains = jnp.stack([g_qa * (ATTN_SCALE * LOG2E), g_ka, g_qb * ATTN_SCALE, g_kb]).astype(F32)
    gains = jnp.pad(gains, ((0, 4), (0, 0)))
    ptiles, stiles = sp_len // tm, ss_len // tm

    def pos_map(i, j):
        return (jnp.where(i < n_p, i % ptiles, (i - n_p) % stiles), 0)

    kern = functools.partial(_inproj_kernel, n_p_tiles=n_p, patterns=patterns)
    return pl.pallas_call(
        kern,
        out_shape=jax.ShapeDtypeStruct((tp + ts, IN_WIDTH), BF16),
        grid=(n_p + n_s, nj),
        in_specs=[
            pl.BlockSpec((tm, D_MODEL), lambda i, j: (jnp.minimum(i, n_p - 1), 0)),
            pl.BlockSpec((tm, D_MODEL), lambda i, j: (jnp.maximum(i - n_p, 0), 0)),
            pl.BlockSpec((1, D_MODEL), lambda i, j: (0, 0)),
            pl.BlockSpec((D_MODEL, tn), lambda i, j: (0, j)),
            pl.BlockSpec((tm, 3 * LANE), pos_map),
            pl.BlockSpec((tm, 3 * LANE), pos_map),
            pl.BlockSpec((8, LANE), lambda i, j: (0, 0)),
        ],
        out_specs=pl.BlockSpec((tm, tn), lambda i, j: (i, j)),
        scratch_shapes=[pltpu.VMEM((tm, D_MODEL), BF16)],
        compiler_params=_cparams(("arbitrary", "arbitrary")),
        name="in_proj",
    )(xp, xs, g_mix.reshape(1, D_MODEL), w_in.astype(BF16), tab_a, tab_b, gains)


def _attn_a_kernel(q_ref, k_ref, v_ref, o_ref, m_ref, l_ref, acc_ref, *, tk, nk):
    m_ref[...] = jnp.full(m_ref.shape, -jnp.inf, F32)
    l_ref[...] = jnp.zeros(l_ref.shape, F32)
    acc_ref[...] = jnp.zeros(acc_ref.shape, F32)

    def body(kj, carry):
        off = pl.multiple_of(kj * tk, tk)
        k = k_ref[pl.ds(off, tk), :]
        v = v_ref[pl.ds(off, tk), :]
        for h in range(A_GROUP):
            q = q_ref[:, h * HEAD_DIM:(h + 1) * HEAD_DIM]
            s = lax.dot_general(q, k, (((1,), (1,)), ((), ())), preferred_element_type=F32)
            m_prev = m_ref[h]
            m_new = jnp.maximum(m_prev, jnp.max(s, axis=-1, keepdims=True))
            alpha = jnp.exp2(m_prev - m_new)
            p = jnp.exp2(s - m_new)
            l_ref[h] = alpha * l_ref[h] + jnp.sum(p, axis=-1, keepdims=True)
            acc_ref[h] = alpha * acc_ref[h] + jnp.dot(p.astype(BF16), v, preferred_element_type=F32)
            m_ref[h] = m_new
        return carry

    lax.fori_loop(0, nk, body, 0)
    for h in range(A_GROUP):
        o_ref[:, h * HEAD_DIM:(h + 1) * HEAD_DIM] = (acc_ref[h] / l_ref[h]).astype(BF16)


def _attn_a(proj, row0, bsz, s):
    tq, tk = TQ_A, min(TK_A, s)
    assert s % tq == 0 and s % tk == 0 and row0 % s == 0
    nq = s // tq
    qw = A_GROUP * HEAD_DIM
    kern = functools.partial(_attn_a_kernel, tk=tk, nk=s // tk)
    return pl.pallas_call(
        kern,
        out_shape=jax.ShapeDtypeStruct((bsz * s, A_Q_W), BF16),
        grid=(bsz, A_KV_HEADS, nq),
        in_specs=[
            pl.BlockSpec((tq, qw), lambda b, g, qi: (row0 // tq + b * nq + qi, COL_QA // qw + g)),
            pl.BlockSpec((s, HEAD_DIM), lambda b, g, qi: (row0 // s + b, COL_KA // HEAD_DIM + g)),
            pl.BlockSpec((s, HEAD_DIM), lambda b, g, qi: (row0 // s + b, COL_VA // HEAD_DIM + g)),
        ],
        out_specs=pl.BlockSpec((tq, qw), lambda b, g, qi: (b * nq + qi, g)),
        scratch_shapes=[pltpu.VMEM((A_GROUP, tq, 1), F32), pltpu.VMEM((A_GROUP, tq, 1), F32),
                        pltpu.VMEM((A_GROUP, tq, HEAD_DIM), F32)],
        compiler_params=_cparams(("arbitrary", "arbitrary", "arbitrary")),
        name="attn_a",
    )(proj, proj, proj)


def _attn_b_kernel(pv_ref, nv_ref, q_ref, kc_ref, kp_ref, kn_ref, vc_ref, vp_ref, vn_ref,
                   o_ref, lse_ref, kw_ref, vw_ref, *, bq):
    ib = pl.program_id(1)
    hw = B_HALF_WINDOW
    kw_ref[0:hw, :] = kp_ref[...]
    kw_ref[hw:hw + bq, :] = kc_ref[...]
    kw_ref[hw + bq:, :] = kn_ref[...]
    vw_ref[0:hw, :] = vp_ref[...]
    vw_ref[hw:hw + bq, :] = vc_ref[...]
    vw_ref[hw + bq:, :] = vn_ref[...]

    wk = bq + 2 * hw
    qi = lax.broadcasted_iota(I32, (bq, wk), 0)
    w = lax.broadcasted_iota(I32, (bq, wk), 1)
    band = jnp.logical_and(w >= qi, w <= qi + 2 * hw)
    lo = jnp.where(pv_ref[ib] > 0, 0, hw)
    hi = jnp.where(nv_ref[ib] > 0, wk, hw + bq)
    mask = jnp.logical_and(band, jnp.logical_and(w >= lo, w < hi))

    for h in range(B_HEADS_PER_GROUP):
        sl = slice(h * HEAD_DIM, (h + 1) * HEAD_DIM)
        s = lax.dot_general(q_ref[:, sl], kw_ref[:, sl], (((1,), (1,)), ((), ())), preferred_element_type=F32)
        s = jnp.where(mask, s, NEG_INF)
        m = jnp.max(s, axis=-1, keepdims=True)
        p = jnp.exp(s - m)
        den = jnp.sum(p, axis=-1, keepdims=True)
        o = jnp.dot(p.astype(BF16), vw_ref[:, sl], preferred_element_type=F32)
        o_ref[:, sl] = o / den
        lse_ref[:, sl] = jnp.broadcast_to(m + jnp.log(den), (bq, HEAD_DIM))


def _attn_b(proj, g, dil, seg_lens_rows, t_total):
    bq, hw = BQ_B, B_HALF_WINDOW
    tf = t_total // dil
    nb = tf // bq
    assert t_total % (dil * bq) == 0
    prev_ok, next_ok = [], []
    for rows, s in seg_lens_rows:
        lf = s // dil
        assert lf % bq == 0
        per_seq = lf // bq
        for _ in range(rows // s):
            for blk in range(per_seq):
                prev_ok.append(int(blk != 0))
                next_ok.append(int(blk != per_seq - 1))
    prev_ok = jnp.asarray(np.array(prev_ok, np.int32))
    next_ok = jnp.asarray(np.array(next_ok, np.int32))
    projf = proj.reshape(tf, dil * IN_WIDTH)
    gw = B_HEADS_PER_GROUP * HEAD_DIM
    cpr = IN_WIDTH // gw
    r = bq // hw
    nh = tf // hw

    def col(base):
        return lambda c, ib, pv, nv: (ib, c * cpr + base // gw + g)

    def col_prev(base):
        return lambda c, ib, pv, nv: (jnp.maximum(ib * r - 1, 0), c * cpr + base // gw + g)

    def col_next(base):
        return lambda c, ib, pv, nv: (jnp.minimum(ib * r + r, nh - 1), c * cpr + base // gw + g)

    grid_spec = pltpu.PrefetchScalarGridSpec(
        num_scalar_prefetch=2,
        grid=(dil, nb),
        in_specs=[
            pl.BlockSpec((bq, gw), col(COL_QB)),
            pl.BlockSpec((bq, gw), col(COL_KB)),
            pl.BlockSpec((hw, gw), col_prev(COL_KB)),
            pl.BlockSpec((hw, gw), col_next(COL_KB)),
            pl.BlockSpec((bq, gw), col(COL_VB)),
            pl.BlockSpec((hw, gw), col_prev(COL_VB)),
            pl.BlockSpec((hw, gw), col_next(COL_VB)),
        ],
        out_specs=[
            pl.BlockSpec((bq, gw), lambda c, ib, pv, nv: (ib, c)),
            pl.BlockSpec((bq, gw), lambda c, ib, pv, nv: (ib, c)),
        ],
        scratch_shapes=[pltpu.VMEM((bq + 2 * hw, gw), BF16), pltpu.VMEM((bq + 2 * hw, gw), BF16)],
    )
    o, lse = pl.pallas_call(
        functools.partial(_attn_b_kernel, bq=bq),
        out_shape=[jax.ShapeDtypeStruct((tf, dil * gw), F32), jax.ShapeDtypeStruct((tf, dil * gw), F32)],
        grid_spec=grid_spec,
        compiler_params=_cparams(("arbitrary", "arbitrary")),
        name=f"attn_b{g}",
    )(prev_ok, next_ok, projf, projf, projf, projf, projf, projf, projf)
    return o.reshape(t_total, gw), lse.reshape(t_total, gw)


def _sigmoid(x):
    return 1.0 / (1.0 + jnp.exp(-x))


def _mix_kernel(ap_ref, as_ref, o0_ref, o1_ref, o2_ref, l0_ref, l1_ref, l2_ref, ga_ref, gb_ref, woa_ref, wob_ref,
                out_ref, ob_ref, a_ref, *, n_p_tiles):
    i = pl.program_id(0)
    first = pl.program_id(1) == 0

    @pl.when(first)
    def _():
        l0, l1, l2 = l0_ref[...], l1_ref[...], l2_ref[...]
        m = jnp.maximum(jnp.maximum(l0, l1), l2)
        e0, e1, e2 = jnp.exp(l0 - m), jnp.exp(l1 - m), jnp.exp(l2 - m)
        o = (e0 * o0_ref[...] + e1 * o1_ref[...] + e2 * o2_ref[...]) / (e0 + e1 + e2)
        ob_ref[...] = o.astype(BF16)

    @pl.when(jnp.logical_and(first, i < n_p_tiles))
    def _():
        a_ref[...] = ap_ref[...]

    @pl.when(jnp.logical_and(first, i >= n_p_tiles))
    def _():
        a_ref[...] = as_ref[...]

    ya = jnp.dot(a_ref[...], woa_ref[...], preferred_element_type=F32)
    yb = jnp.dot(ob_ref[...], wob_ref[...], preferred_element_type=F32)
    mixed = _sigmoid(ga_ref[...].astype(F32)) * ya + _sigmoid(gb_ref[...].astype(F32)) * yb
    out_ref[...] = mixed.astype(BF16)


def _mix(att_p, att_s, ob, lse, proj, w_oa, w_ob):
    tp, ts = att_p.shape[0], att_s.shape[0]
    t = tp + ts
    tm, tn = TM_MIX, TN_MIX
    assert tp % tm == 0 and ts % tm == 0
    n_p = tp // tm
    row = lambda i, j: (i, 0)
    return pl.pallas_call(
        functools.partial(_mix_kernel, n_p_tiles=n_p),
        out_shape=jax.ShapeDtypeStruct((t, D_MODEL), BF16),
        grid=(t // tm, D_MODEL // tn),
        in_specs=[pl.BlockSpec((tm, A_Q_W), lambda i, j: (jnp.minimum(i, n_p - 1), 0)),
                  pl.BlockSpec((tm, A_Q_W), lambda i, j: (jnp.maximum(i - n_p, 0), 0))]
        + [pl.BlockSpec((tm, B_OUT_W), row)] * 6
        + [pl.BlockSpec((tm, tn), lambda i, j: (i, COL_GA // tn + j)),
           pl.BlockSpec((tm, tn), lambda i, j: (i, COL_GB // tn + j)),
           pl.BlockSpec((A_Q_W, tn), lambda i, j: (0, j)),
           pl.BlockSpec((B_OUT_W, tn), lambda i, j: (0, j))],
        out_specs=pl.BlockSpec((tm, tn), lambda i, j: (i, j)),
        scratch_shapes=[pltpu.VMEM((tm, B_OUT_W), BF16), pltpu.VMEM((tm, A_Q_W), BF16)],
        compiler_params=_cparams(("arbitrary", "arbitrary")),
        name="mix",
    )(att_p, att_s, ob[0], ob[1], ob[2], lse[0], lse[1], lse[2], proj, proj, w_oa.astype(BF16), w_ob.astype(BF16))


def _outffn_kernel(xp_ref, xs_ref, mx_ref, wout_ref, gffn_ref, wr_ref, br_ref,
                   x1_ref, ri_ref, rw_ref, cnt_ref, run_ref, *, n_p_tiles, nj, tm, tn):
    i = pl.program_id(0)
    j = pl.program_id(1)
    y = jnp.dot(mx_ref[...], wout_ref[...], preferred_element_type=F32)

    for jj in range(nj):
        @pl.when(jnp.logical_and(j == jj, i < n_p_tiles))
        def _(jj=jj):
            x1_ref[:, jj * tn:(jj + 1) * tn] = xp_ref[...] + y

        @pl.when(jnp.logical_and(j == jj, i >= n_p_tiles))
        def _(jj=jj):
            x1_ref[:, jj * tn:(jj + 1) * tn] = xs_ref[...] + y

    @pl.when(jnp.logical_and(i == 0, j == 0))
    def _():
        run_ref[...] = jnp.zeros(run_ref.shape, F32)

    @pl.when(j == nj - 1)
    def _():
        x1 = x1_ref[...]
        ms = jnp.mean(x1 * x1, axis=-1, keepdims=True)
        h2 = x1 * lax.rsqrt(ms + NORM_EPS) * gffn_ref[...]
        logits = lax.dot_general(wr_ref[...], h2, (((1,), (1,)), ((), ())),
                                 precision=lax.Precision.HIGHEST, preferred_element_type=F32)
        logits = logits + br_ref[:, 0:1]

        row8 = lax.broadcasted_iota(I32, (8, tm), 0)
        lg = jnp.where(row8 < N_EXPERT_GROUPS, logits[0:8], -jnp.inf)
        mg = jnp.max(lg, axis=0, keepdims=True)
        g_sel = jnp.min(jnp.where(lg == mg, row8, 8), axis=0, keepdims=True)
        g_w = 1.0 / jnp.sum(jnp.exp(lg - mg), axis=0, keepdims=True)

        r0 = ROUTER_EXPERT_ROW0
        le = logits[r0:r0 + 8]
        for gi in range(1, N_EXPERT_GROUPS):
            le = jnp.where(g_sel == gi, logits[r0 + 8 * gi:r0 + 8 * gi + 8], le)
        m1 = jnp.max(le, axis=0, keepdims=True)
        i1 = jnp.min(jnp.where(le == m1, row8, 8), axis=0, keepdims=True)
        le2 = jnp.where(row8 == i1, -jnp.inf, le)
        m2 = jnp.max(le2, axis=0, keepdims=True)
        i2 = jnp.min(jnp.where(le2 == m2, row8, 8), axis=0, keepdims=True)
        tt = jnp.exp(m2 - m1)
        p1 = 1.0 / (1.0 + tt)
        p2 = tt / (1.0 + tt)
        e0 = g_sel * EXPERTS_PER_GROUP + i1
        e1 = g_sel * EXPERTS_PER_GROUP + i2

        rowe = lax.broadcasted_iota(I32, (N_EXPERTS, tm), 0)
        oh0 = (rowe == e0).astype(F32)
        oh1 = (rowe == e1).astype(F32)
        cnt = oh0 + oh1
        ti = lax.broadcasted_iota(I32, (tm, tm), 0)
        tj = lax.broadcasted_iota(I32, (tm, tm), 1)
        upper = jnp.where(ti < tj, 1.0, 0.0).astype(BF16)
        pre = jnp.dot(cnt.astype(BF16), upper, preferred_element_type=F32) + run_ref[:, 0:1]
        rank0 = jnp.sum(oh0 * pre, axis=0, keepdims=True)
        rank1 = jnp.sum(oh1 * pre, axis=0, keepdims=True)
        run_ref[...] = run_ref[...] + jnp.sum(cnt, axis=1, keepdims=True)

        ri_ref[...] = jnp.zeros(ri_ref.shape, I32)
        ri_ref[0:1, :] = e0
        ri_ref[1:2, :] = e1
        ri_ref[2:3, :] = rank0.astype(I32)
        ri_ref[3:4, :] = rank1.astype(I32)
        rw_ref[...] = jnp.zeros(rw_ref.shape, F32)
        rw_ref[0:1, :] = g_w * p1
        rw_ref[1:2, :] = g_w * p2
        cnt_ref[...] = run_ref[...].astype(I32)


def _out_ffn(xp, xs, mixed, w_out, g_ffn, w_rg, b_rg, w_re, b_re):
    tp, ts = xp.shape[0], xs.shape[0]
    t = tp + ts
    tm, tn = TM_OUT, TN_OUT
    assert tp % tm == 0 and ts % tm == 0
    n_p = tp // tm
    nj = D_MODEL // tn
    r0 = ROUTER_EXPERT_ROW0
    wr = jnp.zeros((ROUTER_ROWS, D_MODEL), F32)
    wr = wr.at[0:N_EXPERT_GROUPS].set(w_rg.T).at[r0:r0 + N_EXPERTS].set(w_re.T)
    br = jnp.zeros((ROUTER_ROWS,), F32).at[0:N_EXPERT_GROUPS].set(b_rg).at[r0:r0 + N_EXPERTS].set(b_re)
    br = jnp.broadcast_to(br[:, None], (ROUTER_ROWS, LANE))
    kern = functools.partial(_outffn_kernel, n_p_tiles=n_p, nj=nj, tm=tm, tn=tn)
    return pl.pallas_call(
        kern,
        out_shape=[jax.ShapeDtypeStruct((t, D_MODEL), F32),
                   jax.ShapeDtypeStruct((8, t), I32),
                   jax.ShapeDtypeStruct((8, t), F32),
                   jax.ShapeDtypeStruct((N_EXPERTS, LANE), I32)],
        grid=(t // tm, nj),
        in_specs=[
            pl.BlockSpec((tm, tn), lambda i, j: (jnp.minimum(i, n_p - 1), j)),
            pl.BlockSpec((tm, tn), lambda i, j: (jnp.maximum(i - n_p, 0), j)),
            pl.BlockSpec((tm, D_MODEL), lambda i, j: (i, 0)),
            pl.BlockSpec((D_MODEL, tn), lambda i, j: (0, j)),
            pl.BlockSpec((1, D_MODEL), lambda i, j: (0, 0)),
            pl.BlockSpec((ROUTER_ROWS, D_MODEL), lambda i, j: (0, 0)),
            pl.BlockSpec((ROUTER_ROWS, LANE), lambda i, j: (0, 0)),
        ],
        out_specs=[
            pl.BlockSpec((tm, D_MODEL), lambda i, j: (i, 0)),
            pl.BlockSpec((8, tm), lambda i, j: (0, i)),
            pl.BlockSpec((8, tm), lambda i, j: (0, i)),
            pl.BlockSpec((N_EXPERTS, LANE), lambda i, j: (0, 0)),
        ],
        scratch_shapes=[pltpu.VMEM((N_EXPERTS, LANE), F32)],
        compiler_params=_cparams(("arbitrary", "arbitrary")),
        name="out_ffn",
    )(xp, xs, mixed, w_out.astype(BF16), g_ffn.reshape(1, D_MODEL), wr, br)


def _row_copy(src_hbm, dst_buf, sem, src_row, slot, dst_row):
    return pltpu.make_async_copy(src_hbm.at[pl.ds(src_row, 1)], dst_buf.at[slot, pl.ds(dst_row, 1)], sem.at[slot])


def _moe_kernel(blk_e_ref, nused_ref, tok_cur_ref, tok_nxt_ref, x1_hbm, gffn_ref, w1_ref, w3_ref, w2_ref,
                y_ref, xbuf, sem, *, blk):
    i = pl.program_id(0)
    nused = nused_ref[0]
    slot = i % 2

    def issue(tok_ref, s):
        def body(r, c):
            _row_copy(x1_hbm, xbuf, sem, tok_ref[0, 0, r], s, r).start()
            return c
        lax.fori_loop(0, blk, body, 0)

    @pl.when(i == 0)
    def _():
        issue(tok_cur_ref, 0)

    @pl.when(i + 1 < nused)
    def _():
        issue(tok_nxt_ref, 1 - slot)

    @pl.when(i < nused)
    def _():
        def wbody(r, c):
            _row_copy(x1_hbm, xbuf, sem, 0, slot, r).wait()
            return c
        lax.fori_loop(0, blk, wbody, 0)
        x = xbuf[slot]
        ms = jnp.mean(x * x, axis=-1, keepdims=True)
        h = (x * lax.rsqrt(ms + NORM_EPS) * gffn_ref[...]).astype(BF16)
        a = jnp.dot(h, w1_ref[0], preferred_element_type=F32)
        b = jnp.dot(h, w3_ref[0], preferred_element_type=F32)
        act = (a * _sigmoid(a) * b).astype(BF16)
        y_ref[...] = jnp.dot(act, w2_ref[0], preferred_element_type=F32)

    @pl.when(i >= nused)
    def _():
        y_ref[...] = jnp.zeros(y_ref.shape, F32)


def _moe(x1, g_ffn, w1, w3, w2, blk_e, nused, row_tok, nblk):
    blk = MOE_BLK
    tok3 = row_tok.reshape(nblk, 1, blk)
    grid_spec = pltpu.PrefetchScalarGridSpec(
        num_scalar_prefetch=2,
        grid=(nblk,),
        in_specs=[
            pl.BlockSpec((1, 1, blk), lambda i, be, nu: (i, 0, 0), memory_space=pltpu.SMEM),
            pl.BlockSpec((1, 1, blk), lambda i, be, nu: (jnp.minimum(i + 1, nblk - 1), 0, 0), memory_space=pltpu.SMEM),
            pl.BlockSpec(memory_space=pl.ANY),
            pl.BlockSpec((1, D_MODEL), lambda i, be, nu: (0, 0)),
            pl.BlockSpec((1, D_MODEL, D_EXPERT), lambda i, be, nu: (be[i], 0, 0)),
            pl.BlockSpec((1, D_MODEL, D_EXPERT), lambda i, be, nu: (be[i], 0, 0)),
            pl.BlockSpec((1, D_EXPERT, D_MODEL), lambda i, be, nu: (be[i], 0, 0)),
        ],
        out_specs=pl.BlockSpec((blk, D_MODEL), lambda i, be, nu: (i, 0)),
        scratch_shapes=[pltpu.VMEM((2, blk, D_MODEL), F32), pltpu.SemaphoreType.DMA((2,))],
    )
    return pl.pallas_call(
        functools.partial(_moe_kernel, blk=blk),
        out_shape=jax.ShapeDtypeStruct((nblk * blk, D_MODEL), F32),
        grid_spec=grid_spec,
        compiler_params=_cparams(("arbitrary",)),
        name="moe",
    )(blk_e, nused, tok3, tok3, x1, g_ffn.reshape(1, D_MODEL), w1.astype(BF16), w3.astype(BF16), w2.astype(BF16))


def _combine_kernel(d_cur_ref, d_nxt_ref, y_hbm, x1_ref, w0_ref, w1_ref, op_ref, os_ref, ybuf, sem,
                    *, tm, n_p_tiles, nt):
    i = pl.program_id(0)
    slot = i % 2

    def issue(d_ref, s):
        def body(r, c):
            _row_copy(y_hbm, ybuf, sem, d_ref[0, 0, r], s, r).start()
            return c
        lax.fori_loop(0, 2 * tm, body, 0)

    @pl.when(i == 0)
    def _():
        issue(d_cur_ref, 0)

    @pl.when(i + 1 < nt)
    def _():
        issue(d_nxt_ref, 1 - slot)

    def wbody(r, c):
        _row_copy(y_hbm, ybuf, sem, 0, slot, r).wait()
        return c
    lax.fori_loop(0, 2 * tm, wbody, 0)

    w0 = w0_ref[:, 0:1]
    w1 = w1_ref[:, 0:1]
    out = x1_ref[...] + (w0 * ybuf[slot, 0:tm, :] + w1 * ybuf[slot, tm:2 * tm, :])

    @pl.when(i < n_p_tiles)
    def _():
        op_ref[...] = out

    @pl.when(i >= n_p_tiles)
    def _():
        os_ref[...] = out


def _combine(ymoe, x1, dest0, dest1, w0, w1, tp, ts):
    t = tp + ts
    tm = TM_COMB
    assert tp % tm == 0 and ts % tm == 0
    nt = t // tm
    n_p = tp // tm
    dd = jnp.concatenate([dest0.reshape(nt, 1, tm), dest1.reshape(nt, 1, tm)], axis=2)
    w0c = jnp.broadcast_to(w0[:, None], (t, LANE))
    w1c = jnp.broadcast_to(w1[:, None], (t, LANE))
    kern = functools.partial(_combine_kernel, tm=tm, n_p_tiles=n_p, nt=nt)
    return pl.pallas_call(
        kern,
        out_shape=[jax.ShapeDtypeStruct((tp, D_MODEL), F32), jax.ShapeDtypeStruct((ts, D_MODEL), F32)],
        grid=(nt,),
        in_specs=[
            pl.BlockSpec((1, 1, 2 * tm), lambda i: (i, 0, 0), memory_space=pltpu.SMEM),
            pl.BlockSpec((1, 1, 2 * tm), lambda i: (jnp.minimum(i + 1, nt - 1), 0, 0), memory_space=pltpu.SMEM),
            pl.BlockSpec(memory_space=pl.ANY),
            pl.BlockSpec((tm, D_MODEL), lambda i: (i, 0)),
            pl.BlockSpec((tm, LANE), lambda i: (i, 0)),
            pl.BlockSpec((tm, LANE), lambda i: (i, 0)),
        ],
        out_specs=[
            pl.BlockSpec((tm, D_MODEL), lambda i: (jnp.minimum(i, n_p - 1), 0)),
            pl.BlockSpec((tm, D_MODEL), lambda i: (jnp.maximum(i - n_p, 0), 0)),
        ],
        scratch_shapes=[pltpu.VMEM((2, 2 * tm, D_MODEL), F32), pltpu.SemaphoreType.DMA((2,))],
        compiler_params=_cparams(("arbitrary",)),
        name="combine",
    )(dd, dd, ymoe, x1, w0c, w1c)


def _route_plan(ri, counts, n_assign):
    blk = MOE_BLK
    nblk = n_assign // blk + N_EXPERTS
    t = ri.shape[1]
    e0, e1, rank0, rank1 = ri[0], ri[1], ri[2], ri[3]
    cnt = counts[:, 0]
    pcnt = (cnt + blk - 1) // blk * blk
    pend = jnp.cumsum(pcnt)
    pstart = pend - pcnt
    nused = (pend[-1] // blk).astype(I32).reshape(1)
    blk_e = jnp.minimum(jnp.searchsorted(pend, jnp.arange(nblk, dtype=I32) * blk, side="right"),
                        N_EXPERTS - 1).astype(I32)
    oh = lambda e: (e[:, None] == jnp.arange(N_EXPERTS, dtype=I32)[None, :])
    dest0 = jnp.sum(jnp.where(oh(e0), pstart[None, :], 0), axis=1).astype(I32) + rank0
    dest1 = jnp.sum(jnp.where(oh(e1), pstart[None, :], 0), axis=1).astype(I32) + rank1
    tok = jnp.arange(t, dtype=I32)
    row_tok = jnp.zeros((nblk * blk,), I32)
    row_tok = row_tok.at[jnp.concatenate([dest0, dest1])].set(jnp.concatenate([tok, tok]), unique_indices=True)
    return blk_e, nused, row_tok, dest0, dest1, nblk


def _layer(xp, xs, sp_len, ss_len, g_mix, w_in, g_qa, g_ka, g_qb, g_kb, w_oa, w_ob, w_out,
           g_ffn, w_rg, b_rg, w_re, b_re, w1, w3, w2):
    tp, ts = xp.shape[0], xs.shape[0]
    t = tp + ts
    proj = _in_proj(xp, xs, sp_len, ss_len, g_mix, w_in, g_qa, g_ka, g_qb, g_kb)
    att_p = _attn_a(proj, 0, tp // sp_len, sp_len)
    att_s = _attn_a(proj, tp, ts // ss_len, ss_len)
    obs, lses = [], []
    for g, (window, dil) in enumerate(B_PAIRS):
        assert window // (2 * dil) == B_HALF_WINDOW
        o, l = _attn_b(proj, g, dil, [(tp, sp_len), (ts, ss_len)], t)
        obs.append(o)
        lses.append(l)
    mixed = _mix(att_p, att_s, obs, lses, proj, w_oa, w_ob)
    x1, ri, rw, counts = _out_ffn(xp, xs, mixed, w_out, g_ffn, w_rg, b_rg, w_re, b_re)
    blk_e, nused, row_tok, dest0, dest1, nblk = _route_plan(ri, counts, t * TOP_K)
    ymoe = _moe(x1, g_ffn, w1, w3, w2, blk_e, nused, row_tok, nblk)
    return _combine(ymoe, x1, dest0, dest1, rw[0], rw[1], tp, ts)


def kernel(x_prompt, x_sample, g_mix, w_in, g_qa, g_ka, g_qb, g_kb, w_oa, w_ob, w_out, g_ffn, w_rg, b_rg,
           w_re, b_re, w1, w3, w2):
    assert g_mix.shape[0] == 1
    bp, sp_len, d = x_prompt.shape
    bs, ss_len, _ = x_sample.shape
    yp, ys = _layer(x_prompt.reshape(bp * sp_len, d), x_sample.reshape(bs * ss_len, d), sp_len, ss_len,
                    g_mix[0], w_in[0], g_qa[0], g_ka[0], g_qb[0], g_kb[0], w_oa[0], w_ob[0], w_out[0],
                    g_ffn[0], w_rg[0], b_rg[0], w_re[0], b_re[0], w1[0], w3[0], w2[0])
    return yp.reshape(x_prompt.shape), ys.reshape(x_sample.shape)
```

```python
import functools

import jax
import jax.numpy as jnp
import numpy as np
from jax import lax
from jax.experimental import pallas as pl
from jax.experimental.pallas import tpu as pltpu

F32 = jnp.float32
BF16 = jnp.bfloat16
I32 = jnp.int32

D_MODEL = 2048
HEAD_DIM = 128
A_Q_HEADS = 8
A_KV_HEADS = 2
A_GROUP = A_Q_HEADS // A_KV_HEADS
B_PAIRS = ((128, 1), (512, 4), (2048, 16))
B_HEADS_PER_GROUP = 4
B_HALF_WINDOW = 64
GRID_W = 64
AXIAL_THETA = 10000.0
ROPE_THETA = 500000.0
ROPE_DIM = HEAD_DIM // 4
NORM_EPS = 1e-6
NEG_INF = -1e30
ATTN_SCALE = HEAD_DIM ** -0.5
LOG2E = 1.4426950408889634

A_Q_W = A_Q_HEADS * HEAD_DIM
A_KV_W = A_KV_HEADS * HEAD_DIM
B_GROUP_W = B_HEADS_PER_GROUP * HEAD_DIM
B_W = len(B_PAIRS) * B_GROUP_W
IN_WIDTH = A_Q_W + 2 * A_KV_W + 3 * B_W + 2 * D_MODEL
COL_QA = 0
COL_KA = A_Q_W
COL_VA = COL_KA + A_KV_W
COL_QB = COL_VA + A_KV_W
COL_KB = COL_QB + B_W
COL_VB = COL_KB + B_W
COL_GA = COL_VB + B_W

N_EXPERT_GROUPS = 4
EXPERTS_PER_GROUP = 8
N_EXPERTS = N_EXPERT_GROUPS * EXPERTS_PER_GROUP
TOP_K = 2
D_EXPERT = 512
ROUTER_ROWS = 128
ROUTER_EXPERT_ROW0 = 8

LANE = 128
ROW_CHUNKS = D_MODEL // LANE
VMEM_LIMIT = 56 * 1024 * 1024
TM_PROJ = 512
TN_PROJ = 512
TQ_A = 512
TK_A = 512
A_ONES_ROWS = 16
BQ_B = 256
TM_MIX = 512
TN_MIX = 1024
TM_OUT = 512
TN_OUT = 1024
MOE_BLK = 512
TM_COMB = 256
DMA_UNROLL = 8


def _cparams(sem):
    return pltpu.CompilerParams(dimension_semantics=sem, vmem_limit_bytes=VMEM_LIMIT)


def _proj_tiles():
    tiles = [(COL_QA, "qa", "qa"), (COL_QA + TN_PROJ, "qa", "qa"), (COL_KA, "kv", "kv")]
    for g in range(len(B_PAIRS)):
        tiles += [(COL_QB + g * B_GROUP_W, f"b{g}", "qb"), (COL_KB + g * B_GROUP_W, f"b{g}", "kb"),
                  (COL_VB + g * B_GROUP_W, f"b{g}", "plain")]
    tiles += [(COL_GA + c * TN_PROJ, "gate", "plain") for c in range(2 * D_MODEL // TN_PROJ)]
    return tiles


def _inproj_kernel(xp_ref, xs_ref, gmix_ref, w_ref, tqa_ref, tka_ref, tqb_ref, tkb_ref,
                   qa_ref, kv_ref, b0_ref, b1_ref, b2_ref, gate_ref, h_ref, fold_ref, *, n_p_tiles, tiles):
    i = pl.program_id(0)
    j = pl.program_id(1)
    tm = h_ref.shape[0]
    nchunk = TN_PROJ // LANE

    def norm_from(x_ref):
        x = x_ref[...]
        ms = jnp.mean(x * x, axis=-1, keepdims=True)
        h_ref[...] = (x * lax.rsqrt(ms + NORM_EPS) * gmix_ref[...]).astype(BF16)

    @pl.when(jnp.logical_and(j == 0, i < n_p_tiles))
    def _():
        norm_from(xp_ref)

    @pl.when(jnp.logical_and(j == 0, i >= n_p_tiles))
    def _():
        norm_from(xs_ref)

    z = jnp.dot(h_ref[...], w_ref[...], preferred_element_type=F32)

    def qk_chunk(zc, tab_ref, half):
        ms = jnp.mean(zc * zc, axis=-1, keepdims=True)
        rot = (zc * tab_ref[:, 0:LANE] + pltpu.roll(zc, half, 1) * tab_ref[:, LANE:2 * LANE]
               + pltpu.roll(zc, LANE - half, 1) * tab_ref[:, 2 * LANE:3 * LANE])
        return rot * lax.rsqrt(ms + NORM_EPS)

    def chunk(c, cls):
        zc = z[:, c * LANE:(c + 1) * LANE]
        if cls == "qa":
            return qk_chunk(zc, tqa_ref, HEAD_DIM // 4)
        if cls == "ka":
            return qk_chunk(zc, tka_ref, HEAD_DIM // 4)
        if cls == "qb":
            return qk_chunk(zc, tqb_ref, ROPE_DIM // 2)
        if cls == "kb":
            return qk_chunk(zc, tkb_ref, ROPE_DIM // 2)
        return zc

    b_refs = {"b0": (b0_ref, B_PAIRS[0][1]), "b1": (b1_ref, B_PAIRS[1][1]), "b2": (b2_ref, B_PAIRS[2][1])}
    for jt, (_, dest, cls) in enumerate(tiles):
        if dest == "gate" and tiles[jt - 1][1] == "gate":
            continue

        cond = (j >= jt) if dest == "gate" else (j == jt)

        @pl.when(cond)
        def _(dest=dest, cls=cls):
            classes = ["ka", "ka", "plain", "plain"] if cls == "kv" else [cls] * nchunk
            for c in range(nchunk):
                r = chunk(c, classes[c])
                sl = slice(c * LANE, (c + 1) * LANE)
                if dest == "qa":
                    qa_ref[:, sl] = r.astype(BF16)
                elif dest == "kv":
                    kv_ref[:, sl] = r.astype(BF16)
                elif dest == "gate":
                    gate_ref[:, sl] = r.astype(BF16)
                elif b_refs[dest][1] == 1:
                    b_refs[dest][0][0, :, sl] = r.astype(BF16)
                else:
                    fold_ref[c] = r
            if dest in b_refs and b_refs[dest][1] > 1:
                ref, dil = b_refs[dest]
                for res in range(dil):
                    for c in range(nchunk):
                        ref[res, :, c * LANE:(c + 1) * LANE] = (
                            fold_ref[c, pl.ds(res, tm // dil, stride=dil), :].astype(BF16))


def _rope_tables(smax, g_qa, g_ka, g_qb, g_kb):
    t = jnp.arange(smax, dtype=I32)

    def base(blocks, half, theta):
        inv = jnp.power(jnp.float32(theta), -jnp.arange(half, dtype=F32) / half)
        cos_cols, sp_cols, sm_cols = [], [], []
        col = 0
        for pos, start in blocks:
            assert start == col
            ang = pos.astype(F32)[:, None] * inv[None, :]
            cs, sn = jnp.cos(ang), jnp.sin(ang)
            zero = jnp.zeros_like(sn)
            cos_cols += [cs, cs]
            sp_cols += [zero, sn]
            sm_cols += [-sn, zero]
            col += 2 * half
        rest = LANE - col
        cos_cols.append(jnp.ones((smax, rest), F32))
        sp_cols.append(jnp.zeros((smax, rest), F32))
        sm_cols.append(jnp.zeros((smax, rest), F32))
        return (jnp.concatenate(cos_cols, axis=1), jnp.concatenate(sp_cols, axis=1), jnp.concatenate(sm_cols, axis=1))

    def with_gain(tabs, gain, half):
        cos, sp, sm = tabs
        return jnp.concatenate([cos * gain[None, :], sp * jnp.roll(gain, half)[None, :],
                                sm * jnp.roll(gain, -half)[None, :]], axis=1)

    half_a, half_b = HEAD_DIM // 4, ROPE_DIM // 2
    base_a = base([(t // GRID_W, 0), (t % GRID_W, HEAD_DIM // 2)], half_a, AXIAL_THETA)
    base_b = base([(t, 0)], half_b, ROPE_THETA)
    return (with_gain(base_a, g_qa * (ATTN_SCALE * LOG2E), half_a), with_gain(base_a, g_ka, half_a),
            with_gain(base_b, g_qb * ATTN_SCALE, half_b), with_gain(base_b, g_kb, half_b))


def _in_proj(xp, xs, sp_len, ss_len, g_mix, w_in, g_qa, g_ka, g_qb, g_kb):
    tp, ts = xp.shape[0], xs.shape[0]
    t = tp + ts
    tm, tn = TM_PROJ, TN_PROJ
    assert tp % tm == 0 and ts % tm == 0 and sp_len % tm == 0 and ss_len % tm == 0
    n_p, n_s = tp // tm, ts // tm
    tiles = _proj_tiles()
    nj = len(tiles)
    w_perm = jnp.concatenate([w_in[:, c0:c0 + tn] for c0, _, _ in tiles], axis=1).astype(BF16)
    tabs = _rope_tables(max(sp_len, ss_len), g_qa.astype(F32), g_ka.astype(F32), g_qb.astype(F32), g_kb.astype(F32))
    ptiles, stiles = sp_len // tm, ss_len // tm
    first = {}
    for jt, (_, dest, _) in enumerate(tiles):
        first.setdefault(dest, jt)
    count = {d: sum(1 for x in tiles if x[1] == d) for d in first}

    def pos_map(i, j):
        return (jnp.where(i < n_p, i % ptiles, (i - n_p) % stiles), 0)

    def colblk(dest):
        return lambda j: jnp.clip(j - first[dest], 0, count[dest] - 1)

    dils = [d for _, d in B_PAIRS]
    for d in dils:
        assert tm % (d * 16) == 0
    out_shape = [jax.ShapeDtypeStruct((t, A_Q_W), BF16), jax.ShapeDtypeStruct((t, 2 * A_KV_W), BF16)]
    out_specs = [pl.BlockSpec((tm, tn), lambda i, j: (i, colblk("qa")(j))),
                 pl.BlockSpec((tm, tn), lambda i, j: (i, 0))]
    for g, d in enumerate(dils):
        out_shape.append(jax.ShapeDtypeStruct((d, t // d, 3 * B_GROUP_W), BF16))
        out_specs.append(pl.BlockSpec((d, tm // d, tn), lambda i, j, g=g: (0, i, colblk(f"b{g}")(j))))
    out_shape.append(jax.ShapeDtypeStruct((t, 2 * D_MODEL), BF16))
    out_specs.append(pl.BlockSpec((tm, tn), lambda i, j: (i, colblk("gate")(j))))

    kern = functools.partial(_inproj_kernel, n_p_tiles=n_p, tiles=tuple(tiles))
    return pl.pallas_call(
        kern,
        out_shape=out_shape,
        grid=(n_p + n_s, nj),
        in_specs=[
            pl.BlockSpec((tm, D_MODEL), lambda i, j: (jnp.minimum(i, n_p - 1), 0)),
            pl.BlockSpec((tm, D_MODEL), lambda i, j: (jnp.maximum(i - n_p, 0), 0)),
            pl.BlockSpec((1, D_MODEL), lambda i, j: (0, 0)),
            pl.BlockSpec((D_MODEL, tn), lambda i, j: (0, j)),
        ] + [pl.BlockSpec((tm, 3 * LANE), pos_map)] * 4,
        out_specs=out_specs,
        scratch_shapes=[pltpu.VMEM((tm, D_MODEL), BF16), pltpu.VMEM((tn // LANE, tm, LANE), F32)],
        compiler_params=_cparams(("arbitrary", "arbitrary")),
        name="in_proj",
    )(xp, xs, g_mix.reshape(1, D_MODEL), w_perm, *tabs)


def _attn_a_kernel(q_ref, k_ref, v_ref, o_ref, vt_ref, acc_ref, s0_ref, s1_ref, *, tk, nk):
    tq = q_ref.shape[0]

    @pl.when(pl.program_id(2) == 0)
    def _():
        def tbody(c, carry):
            off = pl.multiple_of(c * tk, tk)
            vt_ref[c, 0:HEAD_DIM, :] = v_ref[pl.ds(off, tk), :].astype(F32).T.astype(BF16)
            vt_ref[c, HEAD_DIM:, :] = jnp.ones((A_ONES_ROWS, tk), BF16)
            return carry
        lax.fori_loop(0, nk, tbody, 0)

    acc_ref[...] = jnp.zeros(acc_ref.shape, F32)

    def scores(kj, s_ref):
        off = pl.multiple_of(kj * tk, tk)
        k = k_ref[pl.ds(off, tk), :]
        cmax = []
        for h in range(A_GROUP):
            q = q_ref[:, h * HEAD_DIM:(h + 1) * HEAD_DIM]
            st = lax.dot_general(k, q, (((1,), (1,)), ((), ())), preferred_element_type=F32)
            s_ref[h] = st
            cmax.append(jnp.max(st, axis=0, keepdims=True))
        return tuple(cmax)

    def consume(kj, s_ref, cmax, ms):
        vt = vt_ref[kj]
        new_m = []
        for h in range(A_GROUP):
            m_new = jnp.maximum(ms[h], cmax[h])
            alpha = jnp.exp2(ms[h] - m_new)
            p = jnp.exp2(s_ref[h] - m_new).astype(BF16)
            acc_ref[h] = alpha * acc_ref[h] + jnp.dot(vt, p, preferred_element_type=F32)
            new_m.append(m_new)
        return tuple(new_m)

    def body(jj, carry):
        ms, cm0 = carry
        j0 = 2 * jj
        cm1 = scores(j0 + 1, s1_ref)
        ms = consume(j0, s0_ref, cm0, ms)
        cm2 = scores(jnp.minimum(j0 + 2, nk - 1), s0_ref)
        ms = consume(j0 + 1, s1_ref, cm1, ms)
        return ms, cm2

    m0 = tuple(jnp.full((1, tq), -jnp.inf, F32) for _ in range(A_GROUP))
    lax.fori_loop(0, nk // 2, body, (m0, scores(0, s0_ref)))
    for h in range(A_GROUP):
        acc = acc_ref[h]
        o_ref[:, h * HEAD_DIM:(h + 1) * HEAD_DIM] = (acc[0:HEAD_DIM] / acc[HEAD_DIM:HEAD_DIM + 1]).T.astype(BF16)


def _attn_a(qa, kv, row0, bsz, s):
    tq, tk = TQ_A, min(TK_A, s)
    assert s % tq == 0 and s % (2 * tk) == 0 and row0 % s == 0
    nq = s // tq
    qw = A_GROUP * HEAD_DIM
    kern = functools.partial(_attn_a_kernel, tk=tk, nk=s // tk)
    return pl.pallas_call(
        kern,
        out_shape=jax.ShapeDtypeStruct((bsz * s, A_Q_W), BF16),
        grid=(bsz, A_KV_HEADS, nq),
        in_specs=[
            pl.BlockSpec((tq, qw), lambda b, g, qi: (row0 // tq + b * nq + qi, g)),
            pl.BlockSpec((s, HEAD_DIM), lambda b, g, qi: (row0 // s + b, g)),
            pl.BlockSpec((s, HEAD_DIM), lambda b, g, qi: (row0 // s + b, A_KV_HEADS + g)),
        ],
        out_specs=pl.BlockSpec((tq, qw), lambda b, g, qi: (b * nq + qi, g)),
        scratch_shapes=[pltpu.VMEM((s // tk, HEAD_DIM + A_ONES_ROWS, tk), BF16),
                        pltpu.VMEM((A_GROUP, HEAD_DIM + A_ONES_ROWS, tq), F32),
                        pltpu.VMEM((A_GROUP, tk, tq), F32), pltpu.VMEM((A_GROUP, tk, tq), F32)],
        compiler_params=_cparams(("arbitrary", "arbitrary", "arbitrary")),
        name="attn_a",
    )(qa, kv, kv)


def _attn_b_kernel(pv_ref, nv_ref, q_ref, kc_ref, kp_ref, kn_ref, vc_ref, vp_ref, vn_ref,
                   o_ref, lse_ref, kw_ref, vw_ref, *, bq):
    ib = pl.program_id(1)
    hw = B_HALF_WINDOW
    kw_ref[0:hw, :] = kp_ref[...]
    kw_ref[hw:hw + bq, :] = kc_ref[...]
    kw_ref[hw + bq:, :] = kn_ref[...]
    vw_ref[0:hw, :] = vp_ref[...]
    vw_ref[hw:hw + bq, :] = vc_ref[...]
    vw_ref[hw + bq:, :] = vn_ref[...]

    wk = bq + 2 * hw
    qi = lax.broadcasted_iota(I32, (bq, wk), 0)
    w = lax.broadcasted_iota(I32, (bq, wk), 1)
    band = jnp.logical_and(w >= qi, w <= qi + 2 * hw)
    lo = jnp.where(pv_ref[ib] > 0, 0, hw)
    hi = jnp.where(nv_ref[ib] > 0, wk, hw + bq)
    mask = jnp.logical_and(band, jnp.logical_and(w >= lo, w < hi))

    for h in range(B_HEADS_PER_GROUP):
        sl = slice(h * HEAD_DIM, (h + 1) * HEAD_DIM)
        s = lax.dot_general(q_ref[:, sl], kw_ref[:, sl], (((1,), (1,)), ((), ())), preferred_element_type=F32)
        s = jnp.where(mask, s, NEG_INF)
        m = jnp.max(s, axis=-1, keepdims=True)
        p = jnp.exp(s - m)
        den = jnp.sum(p, axis=-1, keepdims=True)
        o = jnp.dot(p.astype(BF16), vw_ref[:, sl], preferred_element_type=F32)
        o_ref[:, sl] = o / den
        lse_ref[:, sl] = jnp.broadcast_to(m + jnp.log(den), (bq, HEAD_DIM))


def _attn_b(pb, dil, seg_lens_rows, g):
    bq, hw = BQ_B, B_HALF_WINDOW
    tf = pb.shape[1]
    nb = tf // bq
    assert tf % bq == 0
    prev_ok, next_ok = [], []
    for rows, s in seg_lens_rows:
        lf = s // dil
        assert lf % bq == 0
        per_seq = lf // bq
        for _ in range(rows // s):
            for blk in range(per_seq):
                prev_ok.append(int(blk != 0))
                next_ok.append(int(blk != per_seq - 1))
    prev_ok = jnp.asarray(np.array(prev_ok, np.int32))
    next_ok = jnp.asarray(np.array(next_ok, np.int32))
    gw = B_GROUP_W
    r = bq // hw
    nh = tf // hw

    def cur(part):
        return lambda c, ib, pv, nv: (c, ib, part)

    def prev(part):
        return lambda c, ib, pv, nv: (c, jnp.maximum(ib * r - 1, 0), part)

    def nxt(part):
        return lambda c, ib, pv, nv: (c, jnp.minimum(ib * r + r, nh - 1), part)

    grid_spec = pltpu.PrefetchScalarGridSpec(
        num_scalar_prefetch=2,
        grid=(dil, nb),
        in_specs=[
            pl.BlockSpec((None, bq, gw), cur(0)),
            pl.BlockSpec((None, bq, gw), cur(1)),
            pl.BlockSpec((None, hw, gw), prev(1)),
            pl.BlockSpec((None, hw, gw), nxt(1)),
            pl.BlockSpec((None, bq, gw), cur(2)),
            pl.BlockSpec((None, hw, gw), prev(2)),
            pl.BlockSpec((None, hw, gw), nxt(2)),
        ],
        out_specs=[
            pl.BlockSpec((None, bq, gw), lambda c, ib, pv, nv: (c, ib, 0)),
            pl.BlockSpec((None, bq, gw), lambda c, ib, pv, nv: (c, ib, 0)),
        ],
        scratch_shapes=[pltpu.VMEM((bq + 2 * hw, gw), BF16), pltpu.VMEM((bq + 2 * hw, gw), BF16)],
    )
    return pl.pallas_call(
        functools.partial(_attn_b_kernel, bq=bq),
        out_shape=[jax.ShapeDtypeStruct((dil, tf, gw), F32), jax.ShapeDtypeStruct((dil, tf, gw), F32)],
        grid_spec=grid_spec,
        compiler_params=_cparams(("arbitrary", "arbitrary")),
        name=f"attn_b{g}",
    )(prev_ok, next_ok, pb, pb, pb, pb, pb, pb, pb)


def _sigmoid(x):
    return 1.0 / (1.0 + jnp.exp(-x))


def _mix_kernel(ap_ref, as_ref, o0_ref, o1_ref, o2_ref, l0_ref, l1_ref, l2_ref, ga_ref, gb_ref, woa_ref, wob_ref,
                out_ref, ob_ref, a_ref, unf_ref, *, n_p_tiles):
    i = pl.program_id(0)
    first = pl.program_id(1) == 0
    tm = out_ref.shape[0]

    @pl.when(first)
    def _():
        def unfold(ref, slot, sl):
            dil = ref.shape[0]
            for res in range(dil):
                unf_ref[slot, pl.ds(res, tm // dil, stride=dil), :] = ref[res, :, sl]
            return unf_ref[slot]

        for c in range(B_GROUP_W // LANE):
            sl = slice(c * LANE, (c + 1) * LANE)
            l0, l1, l2 = l0_ref[0, :, sl], unfold(l1_ref, 0, sl), unfold(l2_ref, 1, sl)
            m = jnp.maximum(jnp.maximum(l0, l1), l2)
            e0, e1, e2 = jnp.exp(l0 - m), jnp.exp(l1 - m), jnp.exp(l2 - m)
            o = (e0 * o0_ref[0, :, sl] + e1 * unfold(o1_ref, 2, sl) + e2 * unfold(o2_ref, 3, sl)) / (e0 + e1 + e2)
            ob_ref[:, sl] = o.astype(BF16)

    @pl.when(jnp.logical_and(first, i < n_p_tiles))
    def _():
        a_ref[...] = ap_ref[...]

    @pl.when(jnp.logical_and(first, i >= n_p_tiles))
    def _():
        a_ref[...] = as_ref[...]

    ya = jnp.dot(a_ref[...], woa_ref[...], preferred_element_type=F32)
    yb = jnp.dot(ob_ref[...], wob_ref[...], preferred_element_type=F32)
    mixed = _sigmoid(ga_ref[...].astype(F32)) * ya + _sigmoid(gb_ref[...].astype(F32)) * yb
    out_ref[...] = mixed.astype(BF16)


def _mix(att_p, att_s, ob, lse, gates, w_oa, w_ob):
    tp, ts = att_p.shape[0], att_s.shape[0]
    t = tp + ts
    tm, tn = TM_MIX, TN_MIX
    assert tp % tm == 0 and ts % tm == 0
    n_p = tp // tm
    folded = [pl.BlockSpec((d, tm // d, B_GROUP_W), lambda i, j: (0, i, 0)) for _, d in B_PAIRS]
    return pl.pallas_call(
        functools.partial(_mix_kernel, n_p_tiles=n_p),
        out_shape=jax.ShapeDtypeStruct((t, D_MODEL), BF16),
        grid=(t // tm, D_MODEL // tn),
        in_specs=[pl.BlockSpec((tm, A_Q_W), lambda i, j: (jnp.minimum(i, n_p - 1), 0)),
                  pl.BlockSpec((tm, A_Q_W), lambda i, j: (jnp.maximum(i - n_p, 0), 0))]
        + folded + folded
        + [pl.BlockSpec((tm, tn), lambda i, j: (i, j)),
           pl.BlockSpec((tm, tn), lambda i, j: (i, D_MODEL // tn + j)),
           pl.BlockSpec((A_Q_W, tn), lambda i, j: (0, j)),
           pl.BlockSpec((B_GROUP_W, tn), lambda i, j: (0, j))],
        out_specs=pl.BlockSpec((tm, tn), lambda i, j: (i, j)),
        scratch_shapes=[pltpu.VMEM((tm, B_GROUP_W), BF16), pltpu.VMEM((tm, A_Q_W), BF16),
                        pltpu.VMEM((4, tm, LANE), F32)],
        compiler_params=_cparams(("arbitrary", "arbitrary")),
        name="mix",
    )(att_p, att_s, ob[0], ob[1], ob[2], lse[0], lse[1], lse[2], gates, gates, w_oa.astype(BF16), w_ob.astype(BF16))


def _outffn_kernel(xp_ref, xs_ref, mx_ref, wout_ref, gffn_ref, wr_ref, br_ref,
                   x1_ref, ri_ref, rw_ref, cnt_ref, run_ref, x1m_ref, *, n_p_tiles, nj, tm, tn):
    i = pl.program_id(0)
    j = pl.program_id(1)
    y = jnp.dot(mx_ref[...], wout_ref[...], preferred_element_type=F32)
    nchunk = tn // LANE

    def put(jj, val):
        x1m_ref[:, jj * tn:(jj + 1) * tn] = val
        for c in range(nchunk):
            x1_ref[pl.ds(jj * nchunk + c, tm, stride=ROW_CHUNKS), :] = val[:, c * LANE:(c + 1) * LANE]

    for jj in range(nj):
        @pl.when(jnp.logical_and(j == jj, i < n_p_tiles))
        def _(jj=jj):
            put(jj, xp_ref[...] + y)

        @pl.when(jnp.logical_and(j == jj, i >= n_p_tiles))
        def _(jj=jj):
            put(jj, xs_ref[...] + y)

    @pl.when(jnp.logical_and(i == 0, j == 0))
    def _():
        run_ref[...] = jnp.zeros(run_ref.shape, F32)

    @pl.when(j == nj - 1)
    def _():
        x1 = x1m_ref[...]
        ms = jnp.mean(x1 * x1, axis=-1, keepdims=True)
        h2 = x1 * lax.rsqrt(ms + NORM_EPS) * gffn_ref[...]
        logits = lax.dot_general(wr_ref[...], h2, (((1,), (1,)), ((), ())),
                                 precision=lax.Precision.HIGHEST, preferred_element_type=F32)
        logits = logits + br_ref[:, 0:1]

        row8 = lax.broadcasted_iota(I32, (8, tm), 0)
        lg = jnp.where(row8 < N_EXPERT_GROUPS, logits[0:8], -jnp.inf)
        mg = jnp.max(lg, axis=0, keepdims=True)
        g_sel = jnp.min(jnp.where(lg == mg, row8, 8), axis=0, keepdims=True)
        g_w = 1.0 / jnp.sum(jnp.exp(lg - mg), axis=0, keepdims=True)

        r0 = ROUTER_EXPERT_ROW0
        le = logits[r0:r0 + 8]
        for gi in range(1, N_EXPERT_GROUPS):
            le = jnp.where(g_sel == gi, logits[r0 + 8 * gi:r0 + 8 * gi + 8], le)
        m1 = jnp.max(le, axis=0, keepdims=True)
        i1 = jnp.min(jnp.where(le == m1, row8, 8), axis=0, keepdims=True)
        le2 = jnp.where(row8 == i1, -jnp.inf, le)
        m2 = jnp.max(le2, axis=0, keepdims=True)
        i2 = jnp.min(jnp.where(le2 == m2, row8, 8), axis=0, keepdims=True)
        tt = jnp.exp(m2 - m1)
        p1 = 1.0 / (1.0 + tt)
        p2 = tt / (1.0 + tt)
        e0 = g_sel * EXPERTS_PER_GROUP + i1
        e1 = g_sel * EXPERTS_PER_GROUP + i2

        rowe = lax.broadcasted_iota(I32, (N_EXPERTS, tm), 0)
        oh0 = (rowe == e0).astype(F32)
        oh1 = (rowe == e1).astype(F32)
        cnt = oh0 + oh1
        ti = lax.broadcasted_iota(I32, (tm, tm), 0)
        tj = lax.broadcasted_iota(I32, (tm, tm), 1)
        upper = jnp.where(ti < tj, 1.0, 0.0).astype(BF16)
        pre = jnp.dot(cnt.astype(BF16), upper, preferred_element_type=F32) + run_ref[:, 0:1]
        rank0 = jnp.sum(oh0 * pre, axis=0, keepdims=True)
        rank1 = jnp.sum(oh1 * pre, axis=0, keepdims=True)
        run_ref[...] = run_ref[...] + jnp.sum(cnt, axis=1, keepdims=True)

        ri_ref[...] = jnp.zeros(ri_ref.shape, I32)
        ri_ref[0:1, :] = e0
        ri_ref[1:2, :] = e1
        ri_ref[2:3, :] = rank0.astype(I32)
        ri_ref[3:4, :] = rank1.astype(I32)
        rw_ref[...] = jnp.zeros(rw_ref.shape, F32)
        rw_ref[0:1, :] = g_w * p1
        rw_ref[1:2, :] = g_w * p2
        cnt_ref[...] = run_ref[...].astype(I32)


def _out_ffn(xp, xs, mixed, w_out, g_ffn, w_rg, b_rg, w_re, b_re):
    tp, ts = xp.shape[0], xs.shape[0]
    t = tp + ts
    tm, tn = TM_OUT, TN_OUT
    assert tp % tm == 0 and ts % tm == 0
    n_p = tp // tm
    nj = D_MODEL // tn
    r0 = ROUTER_EXPERT_ROW0
    wr = jnp.zeros((ROUTER_ROWS, D_MODEL), F32)
    wr = wr.at[0:N_EXPERT_GROUPS].set(w_rg.T).at[r0:r0 + N_EXPERTS].set(w_re.T)
    br = jnp.zeros((ROUTER_ROWS,), F32).at[0:N_EXPERT_GROUPS].set(b_rg).at[r0:r0 + N_EXPERTS].set(b_re)
    br = jnp.broadcast_to(br[:, None], (ROUTER_ROWS, LANE))
    kern = functools.partial(_outffn_kernel, n_p_tiles=n_p, nj=nj, tm=tm, tn=tn)
    return pl.pallas_call(
        kern,
        out_shape=[jax.ShapeDtypeStruct((t * ROW_CHUNKS, LANE), F32),
                   jax.ShapeDtypeStruct((8, t), I32),
                   jax.ShapeDtypeStruct((8, t), F32),
                   jax.ShapeDtypeStruct((N_EXPERTS, LANE), I32)],
        grid=(t // tm, nj),
        in_specs=[
            pl.BlockSpec((tm, tn), lambda i, j: (jnp.minimum(i, n_p - 1), j)),
            pl.BlockSpec((tm, tn), lambda i, j: (jnp.maximum(i - n_p, 0), j)),
            pl.BlockSpec((tm, D_MODEL), lambda i, j: (i, 0)),
            pl.BlockSpec((D_MODEL, tn), lambda i, j: (0, j)),
            pl.BlockSpec((1, D_MODEL), lambda i, j: (0, 0)),
            pl.BlockSpec((ROUTER_ROWS, D_MODEL), lambda i, j: (0, 0)),
            pl.BlockSpec((ROUTER_ROWS, LANE), lambda i, j: (0, 0)),
        ],
        out_specs=[
            pl.BlockSpec((tm * ROW_CHUNKS, LANE), lambda i, j: (i, 0)),
            pl.BlockSpec((8, tm), lambda i, j: (0, i)),
            pl.BlockSpec((8, tm), lambda i, j: (0, i)),
            pl.BlockSpec((N_EXPERTS, LANE), lambda i, j: (0, 0)),
        ],
        scratch_shapes=[pltpu.VMEM((N_EXPERTS, LANE), F32), pltpu.VMEM((tm, D_MODEL), F32)],
        compiler_params=_cparams(("arbitrary", "arbitrary")),
        name="out_ffn",
    )(xp, xs, mixed, w_out.astype(BF16), g_ffn.reshape(1, D_MODEL), wr, br)


def _gather_rows(src_hbm, idx_ref, buf, sem, slot, n):
    def body(r8, c):
        for u in range(DMA_UNROLL):
            r = r8 * DMA_UNROLL + u
            src = pl.multiple_of(idx_ref[0, 0, r] * ROW_CHUNKS, ROW_CHUNKS)
            dst = pl.multiple_of(r * ROW_CHUNKS, ROW_CHUNKS)
            pltpu.make_async_copy(src_hbm.at[pl.ds(src, ROW_CHUNKS)], buf.at[slot, pl.ds(dst, ROW_CHUNKS)],
                                  sem.at[slot]).start()
        return c
    lax.fori_loop(0, n // DMA_UNROLL, body, 0)


def _wait_rows(src_hbm, buf, sem, slot, n):
    pltpu.make_async_copy(src_hbm.at[pl.ds(0, n * ROW_CHUNKS)], buf.at[slot], sem.at[slot]).wait()


def _lane_chunk(c, rows, row0=0):
    return pl.ds(row0 * ROW_CHUNKS + c, rows, stride=ROW_CHUNKS)


def _moe_kernel(blk_e_ref, nused_ref, tok_cur_ref, tok_nxt_ref, x1_hbm, gffn_ref, w1_ref, w3_ref, w2_ref,
                y_ref, xbuf, sem, hbuf, *, blk):
    i = pl.program_id(0)
    nused = nused_ref[0]
    slot = i % 2

    @pl.when(i == 0)
    def _():
        _gather_rows(x1_hbm, tok_cur_ref, xbuf, sem, 0, blk)

    @pl.when(i + 1 < nused)
    def _():
        _gather_rows(x1_hbm, tok_nxt_ref, xbuf, sem, 1 - slot, blk)

    @pl.when(i < nused)
    def _():
        _wait_rows(x1_hbm, xbuf, sem, slot, blk)
        ssq = jnp.zeros((blk, LANE), F32)
        for c in range(ROW_CHUNKS):
            xc = xbuf[slot, _lane_chunk(c, blk), :]
            ssq = ssq + xc * xc
        rs = lax.rsqrt(jnp.sum(ssq, axis=-1, keepdims=True) * (1.0 / D_MODEL) + NORM_EPS)
        for c in range(ROW_CHUNKS):
            sl = slice(c * LANE, (c + 1) * LANE)
            hbuf[:, sl] = (xbuf[slot, _lane_chunk(c, blk), :] * rs * gffn_ref[:, sl]).astype(BF16)
        h = hbuf[...]
        a = jnp.dot(h, w1_ref[0], preferred_element_type=F32)
        b = jnp.dot(h, w3_ref[0], preferred_element_type=F32)
        act = (a * _sigmoid(a) * b).astype(BF16)
        y = jnp.dot(act, w2_ref[0], preferred_element_type=F32)
        for c in range(ROW_CHUNKS):
            y_ref[_lane_chunk(c, blk), :] = y[:, c * LANE:(c + 1) * LANE]

    @pl.when(i >= nused)
    def _():
        y_ref[...] = jnp.zeros(y_ref.shape, F32)


def _moe(x1, g_ffn, w1, w3, w2, blk_e, nused, row_tok, nblk):
    blk = MOE_BLK
    tok3 = row_tok.reshape(nblk, 1, blk)
    grid_spec = pltpu.PrefetchScalarGridSpec(
        num_scalar_prefetch=2,
        grid=(nblk,),
        in_specs=[
            pl.BlockSpec((1, 1, blk), lambda i, be, nu: (i, 0, 0), memory_space=pltpu.SMEM),
            pl.BlockSpec((1, 1, blk), lambda i, be, nu: (jnp.minimum(i + 1, nblk - 1), 0, 0), memory_space=pltpu.SMEM),
            pl.BlockSpec(memory_space=pl.ANY),
            pl.BlockSpec((1, D_MODEL), lambda i, be, nu: (0, 0)),
            pl.BlockSpec((1, D_MODEL, D_EXPERT), lambda i, be, nu: (be[i], 0, 0)),
            pl.BlockSpec((1, D_MODEL, D_EXPERT), lambda i, be, nu: (be[i], 0, 0)),
            pl.BlockSpec((1, D_EXPERT, D_MODEL), lambda i, be, nu: (be[i], 0, 0)),
        ],
        out_specs=pl.BlockSpec((blk * ROW_CHUNKS, LANE), lambda i, be, nu: (i, 0)),
        scratch_shapes=[pltpu.VMEM((2, blk * ROW_CHUNKS, LANE), F32), pltpu.SemaphoreType.DMA((2,)),
                        pltpu.VMEM((blk, D_MODEL), BF16)],
    )
    return pl.pallas_call(
        functools.partial(_moe_kernel, blk=blk),
        out_shape=jax.ShapeDtypeStruct((nblk * blk * ROW_CHUNKS, LANE), F32),
        grid_spec=grid_spec,
        compiler_params=_cparams(("arbitrary",)),
        name="moe",
    )(blk_e, nused, tok3, tok3, x1, g_ffn.reshape(1, D_MODEL), w1.astype(BF16), w3.astype(BF16), w2.astype(BF16))


def _combine_kernel(d_cur_ref, d_nxt_ref, y_hbm, x1_ref, w0_ref, w1_ref, op_ref, os_ref, ybuf, sem,
                    *, tm, n_p_tiles, nt):
    i = pl.program_id(0)
    slot = i % 2

    @pl.when(i == 0)
    def _():
        _gather_rows(y_hbm, d_cur_ref, ybuf, sem, 0, 2 * tm)

    @pl.when(i + 1 < nt)
    def _():
        _gather_rows(y_hbm, d_nxt_ref, ybuf, sem, 1 - slot, 2 * tm)

    _wait_rows(y_hbm, ybuf, sem, slot, 2 * tm)
    w0 = w0_ref[...]
    w1 = w1_ref[...]

    def write(o_ref):
        for c in range(ROW_CHUNKS):
            o_ref[:, c * LANE:(c + 1) * LANE] = (x1_ref[_lane_chunk(c, tm), :]
                                                 + (w0 * ybuf[slot, _lane_chunk(c, tm), :]
                                                    + w1 * ybuf[slot, _lane_chunk(c, tm, tm), :]))

    @pl.when(i < n_p_tiles)
    def _():
        write(op_ref)

    @pl.when(i >= n_p_tiles)
    def _():
        write(os_ref)


def _combine(ymoe, x1, dest0, dest1, w0, w1, tp, ts):
    t = tp + ts
    tm = TM_COMB
    assert tp % tm == 0 and ts % tm == 0
    nt = t // tm
    n_p = tp // tm
    dd = jnp.concatenate([dest0.reshape(nt, 1, tm), dest1.reshape(nt, 1, tm)], axis=2)
    w0c = jnp.broadcast_to(w0[:, None], (t, LANE))
    w1c = jnp.broadcast_to(w1[:, None], (t, LANE))
    kern = functools.partial(_combine_kernel, tm=tm, n_p_tiles=n_p, nt=nt)
    return pl.pallas_call(
        kern,
        out_shape=[jax.ShapeDtypeStruct((tp, D_MODEL), F32), jax.ShapeDtypeStruct((ts, D_MODEL), F32)],
        grid=(nt,),
        in_specs=[
            pl.BlockSpec((1, 1, 2 * tm), lambda i: (i, 0, 0), memory_space=pltpu.SMEM),
            pl.BlockSpec((1, 1, 2 * tm), lambda i: (jnp.minimum(i + 1, nt - 1), 0, 0), memory_space=pltpu.SMEM),
            pl.BlockSpec(memory_space=pl.ANY),
            pl.BlockSpec((tm * ROW_CHUNKS, LANE), lambda i: (i, 0)),
            pl.BlockSpec((tm, LANE), lambda i: (i, 0)),
            pl.BlockSpec((tm, LANE), lambda i: (i, 0)),
        ],
        out_specs=[
            pl.BlockSpec((tm, D_MODEL), lambda i: (jnp.minimum(i, n_p - 1), 0)),
            pl.BlockSpec((tm, D_MODEL), lambda i: (jnp.maximum(i - n_p, 0), 0)),
        ],
        scratch_shapes=[pltpu.VMEM((2, 2 * tm * ROW_CHUNKS, LANE), F32), pltpu.SemaphoreType.DMA((2,))],
        compiler_params=_cparams(("arbitrary",)),
        name="combine",
    )(dd, dd, ymoe, x1, w0c, w1c)


def _route_plan(ri, counts, n_assign):
    blk = MOE_BLK
    nblk = n_assign // blk + N_EXPERTS
    t = ri.shape[1]
    e0, e1, rank0, rank1 = ri[0], ri[1], ri[2], ri[3]
    cnt = counts[:, 0]
    pcnt = (cnt + blk - 1) // blk * blk
    pend = jnp.cumsum(pcnt)
    pstart = pend - pcnt
    nused = (pend[-1] // blk).astype(I32).reshape(1)
    blk_e = jnp.minimum(jnp.searchsorted(pend, jnp.arange(nblk, dtype=I32) * blk, side="right"),
                        N_EXPERTS - 1).astype(I32)
    oh = lambda e: (e[:, None] == jnp.arange(N_EXPERTS, dtype=I32)[None, :])
    dest0 = jnp.sum(jnp.where(oh(e0), pstart[None, :], 0), axis=1).astype(I32) + rank0
    dest1 = jnp.sum(jnp.where(oh(e1), pstart[None, :], 0), axis=1).astype(I32) + rank1
    tok = jnp.arange(t, dtype=I32)
    row_tok = jnp.zeros((nblk * blk,), I32)
    row_tok = row_tok.at[jnp.concatenate([dest0, dest1])].set(jnp.concatenate([tok, tok]), unique_indices=True)
    return blk_e, nused, row_tok, dest0, dest1, nblk


def _layer(xp, xs, sp_len, ss_len, g_mix, w_in, g_qa, g_ka, g_qb, g_kb, w_oa, w_ob, w_out,
           g_ffn, w_rg, b_rg, w_re, b_re, w1, w3, w2):
    tp, ts = xp.shape[0], xs.shape[0]
    t = tp + ts
    qa, kv, pb0, pb1, pb2, gates = _in_proj(xp, xs, sp_len, ss_len, g_mix, w_in, g_qa, g_ka, g_qb, g_kb)
    att_p = _attn_a(qa, kv, 0, tp // sp_len, sp_len)
    att_s = _attn_a(qa, kv, tp, ts // ss_len, ss_len)
    obs, lses = [], []
    for g, ((window, dil), pb) in enumerate(zip(B_PAIRS, (pb0, pb1, pb2))):
        assert window // (2 * dil) == B_HALF_WINDOW
        o, l = _attn_b(pb, dil, [(tp, sp_len), (ts, ss_len)], g)
        obs.append(o)
        lses.append(l)
    mixed = _mix(att_p, att_s, obs, lses, gates, w_oa, w_ob)
    x1, ri, rw, counts = _out_ffn(xp, xs, mixed, w_out, g_ffn, w_rg, b_rg, w_re, b_re)
    blk_e, nused, row_tok, dest0, dest1, nblk = _route_plan(ri, counts, t * TOP_K)
    ymoe = _moe(x1, g_ffn, w1, w3, w2, blk_e, nused, row_tok, nblk)
    return _combine(ymoe, x1, dest0, dest1, rw[0], rw[1], tp, ts)


def kernel(x_prompt, x_sample, g_mix, w_in, g_qa, g_ka, g_qb, g_kb, w_oa, w_ob, w_out, g_ffn, w_rg, b_rg,
           w_re, b_re, w1, w3, w2):
    assert g_mix.shape[0] == 1
    bp, sp_len, d = x_prompt.shape
    bs, ss_len, _ = x_sample.shape
    yp, ys = _layer(x_prompt.reshape(bp * sp_len, d), x_sample.reshape(bs * ss_len, d), sp_len, ss_len,
                    g_mix[0], w_in[0], g_qa[0], g_ka[0], g_qb[0], g_kb[0], w_oa[0], w_ob[0], w_out[0],
                    g_ffn[0], w_rg[0], b_rg[0], w_re[0], b_re[0], w1[0], w3[0], w2[0])
    return yp.reshape(x_prompt.shape), ys.reshape(x_sample.shape)
```

```python
import functools

import jax
import jax.numpy as jnp
import numpy as np
from jax import lax
from jax.experimental import pallas as pl
from jax.experimental.pallas import tpu as pltpu

F32 = jnp.float32
BF16 = jnp.bfloat16
I32 = jnp.int32

D_MODEL = 2048
HEAD_DIM = 128
A_Q_HEADS = 8
A_KV_HEADS = 2
A_GROUP = A_Q_HEADS // A_KV_HEADS
B_PAIRS = ((128, 1), (512, 4), (2048, 16))
B_HEADS_PER_GROUP = 4
B_HALF_WINDOW = 64
GRID_W = 64
AXIAL_THETA = 10000.0
ROPE_THETA = 500000.0
ROPE_DIM = HEAD_DIM // 4
NORM_EPS = 1e-6
NEG_INF = -1e30
ATTN_SCALE = HEAD_DIM ** -0.5
LOG2E = 1.4426950408889634

A_Q_W = A_Q_HEADS * HEAD_DIM
A_KV_W = A_KV_HEADS * HEAD_DIM
B_GROUP_W = B_HEADS_PER_GROUP * HEAD_DIM
B_W = len(B_PAIRS) * B_GROUP_W
IN_WIDTH = A_Q_W + 2 * A_KV_W + 3 * B_W + 2 * D_MODEL
COL_QA = 0
COL_KA = A_Q_W
COL_VA = COL_KA + A_KV_W
COL_QB = COL_VA + A_KV_W
COL_KB = COL_QB + B_W
COL_VB = COL_KB + B_W
COL_GA = COL_VB + B_W

N_EXPERT_GROUPS = 4
EXPERTS_PER_GROUP = 8
N_EXPERTS = N_EXPERT_GROUPS * EXPERTS_PER_GROUP
TOP_K = 2
D_EXPERT = 512
ROUTER_ROWS = 128
ROUTER_EXPERT_ROW0 = 8

LANE = 128
ROW_CHUNKS = D_MODEL // LANE
VMEM_LIMIT = 56 * 1024 * 1024
TM_PROJ = 1024
TN_PROJ = 512
TQ_A = 1024
TK_A = 512
A_ONES_ROWS = 16
BQ_B = 256
TM_MIX = 512
TN_MIX = 1024
TM_OUT = 512
TN_OUT = 1024
MOE_BLK = 512
TM_COMB = 256
DMA_UNROLL = 8


def _cparams(sem):
    return pltpu.CompilerParams(dimension_semantics=sem, vmem_limit_bytes=VMEM_LIMIT)


def _proj_tiles():
    tiles = [(COL_QA, "qa", "qa"), (COL_QA + TN_PROJ, "qa", "qa"), (COL_KA, "kv", "kv")]
    for g in range(len(B_PAIRS)):
        tiles += [(COL_QB + g * B_GROUP_W, f"b{g}", "qb"), (COL_KB + g * B_GROUP_W, f"b{g}", "kb"),
                  (COL_VB + g * B_GROUP_W, f"b{g}", "plain")]
    tiles += [(COL_GA + c * TN_PROJ, "gate", "plain") for c in range(2 * D_MODEL // TN_PROJ)]
    return tiles


GAIN_ROWS = {"qa": 0, "ka": 1, "qb": 2, "kb": 3}


def _inproj_kernel(xp_hbm, xs_hbm, gmix_ref, w_ref, taba_ref, tabb_ref, gains_ref,
                   qa_ref, kv_ref, b0_ref, b1_ref, b2_ref, gate_ref,
                   xbuf, xsem, h_ref, z0_ref, z1_ref, fold_ref, *, n_p_tiles, n_tiles, tiles):
    i = pl.program_id(0)
    j = pl.program_id(1)
    tm = h_ref.shape[0]
    nchunk = TN_PROJ // LANE
    nj = len(tiles)
    zbufs = (z0_ref, z1_ref)

    def x_copy(tile, start):
        def go(src, first_row):
            cp = pltpu.make_async_copy(src.at[pl.ds(pl.multiple_of(first_row, tm), tm)], xbuf, xsem.at[0])
            cp.start() if start else cp.wait()

        @pl.when(tile < n_p_tiles)
        def _():
            go(xp_hbm, tile * tm)

        @pl.when(tile >= n_p_tiles)
        def _():
            go(xs_hbm, (tile - n_p_tiles) * tm)

    @pl.when(j == 0)
    def _():
        @pl.when(i == 0)
        def _():
            x_copy(i, True)

        x_copy(i, False)
        x = xbuf[...]
        ms = jnp.mean(x * x, axis=-1, keepdims=True)
        h_ref[...] = (x * lax.rsqrt(ms + NORM_EPS) * gmix_ref[...]).astype(BF16)

        @pl.when(i + 1 < n_tiles)
        def _():
            x_copy(i + 1, True)

    def qk_chunk(zc, cls):
        tab_ref, half = (taba_ref, HEAD_DIM // 4) if cls in ("qa", "ka") else (tabb_ref, ROPE_DIM // 2)
        ms = jnp.mean(zc * zc, axis=-1, keepdims=True)
        row = GAIN_ROWS[cls]
        zg = zc * gains_ref[row:row + 1, :]
        rot = (zg * tab_ref[:, 0:LANE] + pltpu.roll(zg, half, 1) * tab_ref[:, LANE:2 * LANE]
               + pltpu.roll(zg, LANE - half, 1) * tab_ref[:, 2 * LANE:3 * LANE])
        return rot * lax.rsqrt(ms + NORM_EPS)

    b_refs = {"b0": (b0_ref, B_PAIRS[0][1]), "b1": (b1_ref, B_PAIRS[1][1]), "b2": (b2_ref, B_PAIRS[2][1])}

    def epilogue(jt):
        _, dest, cls = tiles[jt]
        z_ref = zbufs[jt % 2]
        classes = ["ka", "ka", "plain", "plain"] if cls == "kv" else [cls] * nchunk
        for c in range(nchunk):
            sl = slice(c * LANE, (c + 1) * LANE)
            zc = z_ref[:, sl]
            r = zc if classes[c] == "plain" else qk_chunk(zc, classes[c])
            if dest == "qa":
                qa_ref[:, sl] = r.astype(BF16)
            elif dest == "kv":
                kv_ref[:, sl] = r.astype(BF16)
            elif dest == "gate":
                gate_ref[:, sl] = r.astype(BF16)
            elif b_refs[dest][1] == 1:
                b_refs[dest][0][0, :, sl] = r.astype(BF16)
            else:
                fold_ref[c] = r
        if dest in b_refs and b_refs[dest][1] > 1:
            ref, dil = b_refs[dest]
            for res in range(dil):
                for c in range(nchunk):
                    ref[res, :, c * LANE:(c + 1) * LANE] = (
                        fold_ref[c, pl.ds(res, tm // dil, stride=dil), :].astype(BF16))

    def step(jt):
        if jt < nj:
            zbufs[jt % 2][...] = jnp.dot(h_ref[...], w_ref[...], preferred_element_type=F32)
        if jt >= 1:
            epilogue(jt - 1)

    first_gate = min(jt for jt, tl in enumerate(tiles) if tl[1] == "gate")
    for jt in range(nj + 1):
        if first_gate + 1 <= jt < nj:
            if jt >= first_gate + 3:
                continue
            cond = jnp.logical_and(jnp.logical_and(j >= jt, j < nj), (j - jt) % 2 == 0)
        else:
            cond = j == jt
        pl.when(cond)(functools.partial(step, jt))


def _rope_tables(smax):
    t = jnp.arange(smax, dtype=I32)

    def base(blocks, half, theta):
        inv = jnp.power(jnp.float32(theta), -jnp.arange(half, dtype=F32) / half)
        cos_cols, sp_cols, sm_cols = [], [], []
        col = 0
        for pos, start in blocks:
            assert start == col
            ang = pos.astype(F32)[:, None] * inv[None, :]
            cs, sn = jnp.cos(ang), jnp.sin(ang)
            zero = jnp.zeros_like(sn)
            cos_cols += [cs, cs]
            sp_cols += [zero, sn]
            sm_cols += [-sn, zero]
            col += 2 * half
        rest = LANE - col
        cos_cols.append(jnp.ones((smax, rest), F32))
        sp_cols.append(jnp.zeros((smax, rest), F32))
        sm_cols.append(jnp.zeros((smax, rest), F32))
        return jnp.concatenate(cos_cols + sp_cols + sm_cols, axis=1)

    return (base([(t // GRID_W, 0), (t % GRID_W, HEAD_DIM // 2)], HEAD_DIM // 4, AXIAL_THETA),
            base([(t, 0)], ROPE_DIM // 2, ROPE_THETA))


def _in_proj(xp, xs, sp_len, ss_len, g_mix, w_in, g_qa, g_ka, g_qb, g_kb):
    tp, ts = xp.shape[0], xs.shape[0]
    t = tp + ts
    tm, tn = TM_PROJ, TN_PROJ
    assert tp % tm == 0 and ts % tm == 0 and sp_len % tm == 0 and ss_len % tm == 0
    n_p, n_s = tp // tm, ts // tm
    tiles = _proj_tiles()
    nj = len(tiles)
    w_perm = jnp.concatenate([w_in[:, c0:c0 + tn] for c0, _, _ in tiles], axis=1).astype(BF16)
    tab_a, tab_b = _rope_tables(max(sp_len, ss_len))
    gains = jnp.stack([g_qa * (ATTN_SCALE * LOG2E), g_ka, g_qb * ATTN_SCALE, g_kb]).astype(F32)
    gains = jnp.pad(gains, ((0, 8 - gains.shape[0]), (0, 0)))
    ptiles, stiles = sp_len // tm, ss_len // tm
    first = {}
    for jt, (_, dest, _) in enumerate(tiles):
        first.setdefault(dest, jt)
    count = {d: sum(1 for x in tiles if x[1] == d) for d in first}

    def pos_map(i, j):
        return (jnp.where(i < n_p, i % ptiles, (i - n_p) % stiles), 0)

    def colblk(dest):
        return lambda j: jnp.clip(j - 1 - first[dest], 0, count[dest] - 1)

    dils = [d for _, d in B_PAIRS]
    for d in dils:
        assert tm % (d * 16) == 0
    out_shape = [jax.ShapeDtypeStruct((t, A_Q_W), BF16), jax.ShapeDtypeStruct((t, 2 * A_KV_W), BF16)]
    out_specs = [pl.BlockSpec((tm, tn), lambda i, j: (i, colblk("qa")(j))),
                 pl.BlockSpec((tm, tn), lambda i, j: (i, 0))]
    for g, d in enumerate(dils):
        out_shape.append(jax.ShapeDtypeStruct((d, t // d, 3 * B_GROUP_W), BF16))
        out_specs.append(pl.BlockSpec((d, tm // d, tn), lambda i, j, g=g: (0, i, colblk(f"b{g}")(j))))
    out_shape.append(jax.ShapeDtypeStruct((t, 2 * D_MODEL), BF16))
    out_specs.append(pl.BlockSpec((tm, tn), lambda i, j: (i, colblk("gate")(j))))

    kern = functools.partial(_inproj_kernel, n_p_tiles=n_p, n_tiles=n_p + n_s, tiles=tuple(tiles))
    return pl.pallas_call(
        kern,
        out_shape=out_shape,
        grid=(n_p + n_s, nj + 1),
        in_specs=[
            pl.BlockSpec(memory_space=pl.ANY),
            pl.BlockSpec(memory_space=pl.ANY),
            pl.BlockSpec((1, D_MODEL), lambda i, j: (0, 0)),
            pl.BlockSpec((D_MODEL, tn), lambda i, j: (0, jnp.minimum(j, nj - 1))),
            pl.BlockSpec((tm, 3 * LANE), pos_map),
            pl.BlockSpec((tm, 3 * LANE), pos_map),
            pl.BlockSpec((8, LANE), lambda i, j: (0, 0)),
        ],
        out_specs=out_specs,
        scratch_shapes=[pltpu.VMEM((tm, D_MODEL), F32), pltpu.SemaphoreType.DMA((1,)),
                        pltpu.VMEM((tm, D_MODEL), BF16),
                        pltpu.VMEM((tm, tn), F32), pltpu.VMEM((tm, tn), F32),
                        pltpu.VMEM((tn // LANE, tm, LANE), F32)],
        compiler_params=_cparams(("arbitrary", "arbitrary")),
        name="in_proj",
    )(xp, xs, g_mix.reshape(1, D_MODEL), w_perm, tab_a, tab_b, gains)


def _attn_a_kernel(q_ref, k_ref, v_ref, o_ref, vt_ref, acc_ref, s0_ref, s1_ref, *, tk, nk):
    tq = q_ref.shape[0]

    @pl.when(pl.program_id(2) == 0)
    def _():
        def tbody(c, carry):
            off = pl.multiple_of(c * tk, tk)
            vt_ref[c, 0:HEAD_DIM, :] = v_ref[pl.ds(off, tk), :].astype(F32).T.astype(BF16)
            vt_ref[c, HEAD_DIM:, :] = jnp.ones((A_ONES_ROWS, tk), BF16)
            return carry
        lax.fori_loop(0, nk, tbody, 0)

    acc_ref[...] = jnp.zeros(acc_ref.shape, F32)

    def scores(kj, s_ref):
        off = pl.multiple_of(kj * tk, tk)
        k = k_ref[pl.ds(off, tk), :]
        cmax = []
        for h in range(A_GROUP):
            q = q_ref[:, h * HEAD_DIM:(h + 1) * HEAD_DIM]
            st = lax.dot_general(k, q, (((1,), (1,)), ((), ())), preferred_element_type=F32)
            s_ref[h] = st
            cmax.append(jnp.max(st, axis=0, keepdims=True))
        return tuple(cmax)

    def consume(kj, s_ref, cmax, ms):
        vt = vt_ref[kj]
        new_m = []
        for h in range(A_GROUP):
            m_new = jnp.maximum(ms[h], cmax[h])
            alpha = jnp.exp2(ms[h] - m_new)
            p = jnp.exp2(s_ref[h] - m_new).astype(BF16)
            acc_ref[h] = alpha * acc_ref[h] + jnp.dot(vt, p, preferred_element_type=F32)
            new_m.append(m_new)
        return tuple(new_m)

    def body(jj, carry):
        ms, cm0 = carry
        j0 = 2 * jj
        cm1 = scores(j0 + 1, s1_ref)
        ms = consume(j0, s0_ref, cm0, ms)
        cm2 = scores(jnp.minimum(j0 + 2, nk - 1), s0_ref)
        ms = consume(j0 + 1, s1_ref, cm1, ms)
        return ms, cm2

    m0 = tuple(jnp.full((1, tq), -jnp.inf, F32) for _ in range(A_GROUP))
    lax.fori_loop(0, nk // 2, body, (m0, scores(0, s0_ref)))
    for h in range(A_GROUP):
        acc = acc_ref[h]
        o_ref[:, h * HEAD_DIM:(h + 1) * HEAD_DIM] = (acc[0:HEAD_DIM] / acc[HEAD_DIM:HEAD_DIM + 1]).T.astype(BF16)


def _attn_a(qa, kv, row0, bsz, s):
    tq, tk = TQ_A, min(TK_A, s)
    assert s % tq == 0 and s % (2 * tk) == 0 and row0 % s == 0
    nq = s // tq
    qw = A_GROUP * HEAD_DIM
    kern = functools.partial(_attn_a_kernel, tk=tk, nk=s // tk)
    return pl.pallas_call(
        kern,
        out_shape=jax.ShapeDtypeStruct((bsz * s, A_Q_W), BF16),
        grid=(bsz, A_KV_HEADS, nq),
        in_specs=[
            pl.BlockSpec((tq, qw), lambda b, g, qi: (row0 // tq + b * nq + qi, g)),
            pl.BlockSpec((s, HEAD_DIM), lambda b, g, qi: (row0 // s + b, g)),
            pl.BlockSpec((s, HEAD_DIM), lambda b, g, qi: (row0 // s + b, A_KV_HEADS + g)),
        ],
        out_specs=pl.BlockSpec((tq, qw), lambda b, g, qi: (b * nq + qi, g)),
        scratch_shapes=[pltpu.VMEM((s // tk, HEAD_DIM + A_ONES_ROWS, tk), BF16),
                        pltpu.VMEM((A_GROUP, HEAD_DIM + A_ONES_ROWS, tq), F32),
                        pltpu.VMEM((A_GROUP, tk, tq), F32), pltpu.VMEM((A_GROUP, tk, tq), F32)],
        compiler_params=_cparams(("arbitrary", "arbitrary", "arbitrary")),
        name="attn_a",
    )(qa, kv, kv)


def _attn_b_kernel(pv_ref, nv_ref, q_ref, kc_ref, kp_ref, kn_ref, vc_ref, vp_ref, vn_ref,
                   o_ref, lse_ref, kw_ref, vw_ref, *, bq):
    ib = pl.program_id(1)
    hw = B_HALF_WINDOW
    kw_ref[0:hw, :] = kp_ref[...]
    kw_ref[hw:hw + bq, :] = kc_ref[...]
    kw_ref[hw + bq:, :] = kn_ref[...]
    vw_ref[0:hw, :] = vp_ref[...]
    vw_ref[hw:hw + bq, :] = vc_ref[...]
    vw_ref[hw + bq:, :] = vn_ref[...]

    wk = bq + 2 * hw
    qi = lax.broadcasted_iota(I32, (bq, wk), 0)
    w = lax.broadcasted_iota(I32, (bq, wk), 1)
    band = jnp.logical_and(w >= qi, w <= qi + 2 * hw)
    lo = jnp.where(pv_ref[ib] > 0, 0, hw)
    hi = jnp.where(nv_ref[ib] > 0, wk, hw + bq)
    mask = jnp.logical_and(band, jnp.logical_and(w >= lo, w < hi))

    for h in range(B_HEADS_PER_GROUP):
        sl = slice(h * HEAD_DIM, (h + 1) * HEAD_DIM)
        s = lax.dot_general(q_ref[:, sl], kw_ref[:, sl], (((1,), (1,)), ((), ())), preferred_element_type=F32)
        s = jnp.where(mask, s, NEG_INF)
        m = jnp.max(s, axis=-1, keepdims=True)
        p = jnp.exp(s - m)
        den = jnp.sum(p, axis=-1, keepdims=True)
        o = jnp.dot(p.astype(BF16), vw_ref[:, sl], preferred_element_type=F32)
        o_ref[:, sl] = o / den
        lse_ref[:, sl] = jnp.broadcast_to(m + jnp.log(den), (bq, HEAD_DIM))


def _attn_b(pb, dil, seg_lens_rows, g):
    bq, hw = BQ_B, B_HALF_WINDOW
    tf = pb.shape[1]
    nb = tf // bq
    assert tf % bq == 0
    prev_ok, next_ok = [], []
    for rows, s in seg_lens_rows:
        lf = s // dil
        assert lf % bq == 0
        per_seq = lf // bq
        for _ in range(rows // s):
            for blk in range(per_seq):
                prev_ok.append(int(blk != 0))
                next_ok.append(int(blk != per_seq - 1))
    prev_ok = jnp.asarray(np.array(prev_ok, np.int32))
    next_ok = jnp.asarray(np.array(next_ok, np.int32))
    gw = B_GROUP_W
    r = bq // hw
    nh = tf // hw

    def cur(part):
        return lambda c, ib, pv, nv: (c, ib, part)

    def prev(part):
        return lambda c, ib, pv, nv: (c, jnp.maximum(ib * r - 1, 0), part)

    def nxt(part):
        return lambda c, ib, pv, nv: (c, jnp.minimum(ib * r + r, nh - 1), part)

    grid_spec = pltpu.PrefetchScalarGridSpec(
        num_scalar_prefetch=2,
        grid=(dil, nb),
        in_specs=[
            pl.BlockSpec((None, bq, gw), cur(0)),
            pl.BlockSpec((None, bq, gw), cur(1)),
            pl.BlockSpec((None, hw, gw), prev(1)),
            pl.BlockSpec((None, hw, gw), nxt(1)),
            pl.BlockSpec((None, bq, gw), cur(2)),
            pl.BlockSpec((None, hw, gw), prev(2)),
            pl.BlockSpec((None, hw, gw), nxt(2)),
        ],
        out_specs=[
            pl.BlockSpec((None, bq, gw), lambda c, ib, pv, nv: (c, ib, 0)),
            pl.BlockSpec((None, bq, gw), lambda c, ib, pv, nv: (c, ib, 0)),
        ],
        scratch_shapes=[pltpu.VMEM((bq + 2 * hw, gw), BF16), pltpu.VMEM((bq + 2 * hw, gw), BF16)],
    )
    return pl.pallas_call(
        functools.partial(_attn_b_kernel, bq=bq),
        out_shape=[jax.ShapeDtypeStruct((dil, tf, gw), F32), jax.ShapeDtypeStruct((dil, tf, gw), F32)],
        grid_spec=grid_spec,
        compiler_params=_cparams(("arbitrary", "arbitrary")),
        name=f"attn_b{g}",
    )(prev_ok, next_ok, pb, pb, pb, pb, pb, pb, pb)


def _sigmoid(x):
    return 1.0 / (1.0 + jnp.exp(-x))


def _mix_kernel(ap_ref, as_ref, o0_ref, o1_ref, o2_ref, l0_ref, l1_ref, l2_ref, ga_ref, gb_ref, woa_ref, wob_ref,
                out_ref, ob_ref, a_ref, unf_ref, *, n_p_tiles):
    i = pl.program_id(0)
    first = pl.program_id(1) == 0
    tm = out_ref.shape[0]

    @pl.when(first)
    def _():
        def unfold(ref, slot, sl):
            dil = ref.shape[0]
            for res in range(dil):
                unf_ref[slot, pl.ds(res, tm // dil, stride=dil), :] = ref[res, :, sl]
            return unf_ref[slot]

        for c in range(B_GROUP_W // LANE):
            sl = slice(c * LANE, (c + 1) * LANE)
            l0, l1, l2 = l0_ref[0, :, sl], unfold(l1_ref, 0, sl), unfold(l2_ref, 1, sl)
            m = jnp.maximum(jnp.maximum(l0, l1), l2)
            e0, e1, e2 = jnp.exp(l0 - m), jnp.exp(l1 - m), jnp.exp(l2 - m)
            o = (e0 * o0_ref[0, :, sl] + e1 * unfold(o1_ref, 2, sl) + e2 * unfold(o2_ref, 3, sl)) / (e0 + e1 + e2)
            ob_ref[:, sl] = o.astype(BF16)

    @pl.when(jnp.logical_and(first, i < n_p_tiles))
    def _():
        a_ref[...] = ap_ref[...]

    @pl.when(jnp.logical_and(first, i >= n_p_tiles))
    def _():
        a_ref[...] = as_ref[...]

    ya = jnp.dot(a_ref[...], woa_ref[...], preferred_element_type=F32)
    yb = jnp.dot(ob_ref[...], wob_ref[...], preferred_element_type=F32)
    mixed = _sigmoid(ga_ref[...].astype(F32)) * ya + _sigmoid(gb_ref[...].astype(F32)) * yb
    out_ref[...] = mixed.astype(BF16)


def _mix(att_p, att_s, ob, lse, gates, w_oa, w_ob):
    tp, ts = att_p.shape[0], att_s.shape[0]
    t = tp + ts
    tm, tn = TM_MIX, TN_MIX
    assert tp % tm == 0 and ts % tm == 0
    n_p = tp // tm
    folded = [pl.BlockSpec((d, tm // d, B_GROUP_W), lambda i, j: (0, i, 0)) for _, d in B_PAIRS]
    return pl.pallas_call(
        functools.partial(_mix_kernel, n_p_tiles=n_p),
        out_shape=jax.ShapeDtypeStruct((t, D_MODEL), BF16),
        grid=(t // tm, D_MODEL // tn),
        in_specs=[pl.BlockSpec((tm, A_Q_W), lambda i, j: (jnp.minimum(i, n_p - 1), 0)),
                  pl.BlockSpec((tm, A_Q_W), lambda i, j: (jnp.maximum(i - n_p, 0), 0))]
        + folded + folded
        + [pl.BlockSpec((tm, tn), lambda i, j: (i, j)),
           pl.BlockSpec((tm, tn), lambda i, j: (i, D_MODEL // tn + j)),
           pl.BlockSpec((A_Q_W, tn), lambda i, j: (0, j)),
           pl.BlockSpec((B_GROUP_W, tn), lambda i, j: (0, j))],
        out_specs=pl.BlockSpec((tm, tn), lambda i, j: (i, j)),
        scratch_shapes=[pltpu.VMEM((tm, B_GROUP_W), BF16), pltpu.VMEM((tm, A_Q_W), BF16),
                        pltpu.VMEM((4, tm, LANE), F32)],
        compiler_params=_cparams(("arbitrary", "arbitrary")),
        name="mix",
    )(att_p, att_s, ob[0], ob[1], ob[2], lse[0], lse[1], lse[2], gates, gates, w_oa.astype(BF16), w_ob.astype(BF16))


def _outffn_kernel(xp_ref, xs_ref, mx_ref, wout_ref, gffn_ref, wr_ref, br_ref,
                   x1_ref, ri_ref, rw_ref, cnt_ref, run_ref, x1m_ref, *, n_p_tiles, nj, tm, tn):
    i = pl.program_id(0)
    j = pl.program_id(1)
    y = jnp.dot(mx_ref[...], wout_ref[...], preferred_element_type=F32)
    nchunk = tn // LANE

    def put(jj, val):
        x1m_ref[:, jj * tn:(jj + 1) * tn] = val
        for c in range(nchunk):
            x1_ref[pl.ds(jj * nchunk + c, tm, stride=ROW_CHUNKS), :] = val[:, c * LANE:(c + 1) * LANE]

    for jj in range(nj):
        @pl.when(jnp.logical_and(j == jj, i < n_p_tiles))
        def _(jj=jj):
            put(jj, xp_ref[...] + y)

        @pl.when(jnp.logical_and(j == jj, i >= n_p_tiles))
        def _(jj=jj):
            put(jj, xs_ref[...] + y)

    @pl.when(jnp.logical_and(i == 0, j == 0))
    def _():
        run_ref[...] = jnp.zeros(run_ref.shape, F32)

    @pl.when(j == nj - 1)
    def _():
        x1 = x1m_ref[...]
        ms = jnp.mean(x1 * x1, axis=-1, keepdims=True)
        h2 = x1 * lax.rsqrt(ms + NORM_EPS) * gffn_ref[...]
        logits = lax.dot_general(wr_ref[...], h2, (((1,), (1,)), ((), ())),
                                 precision=lax.Precision.HIGHEST, preferred_element_type=F32)
        logits = logits + br_ref[:, 0:1]

        row8 = lax.broadcasted_iota(I32, (8, tm), 0)
        lg = jnp.where(row8 < N_EXPERT_GROUPS, logits[0:8], -jnp.inf)
        mg = jnp.max(lg, axis=0, keepdims=True)
        g_sel = jnp.min(jnp.where(lg == mg, row8, 8), axis=0, keepdims=True)
        g_w = 1.0 / jnp.sum(jnp.exp(lg - mg), axis=0, keepdims=True)

        r0 = ROUTER_EXPERT_ROW0
        le = logits[r0:r0 + 8]
        for gi in range(1, N_EXPERT_GROUPS):
            le = jnp.where(g_sel == gi, logits[r0 + 8 * gi:r0 + 8 * gi + 8], le)
        m1 = jnp.max(le, axis=0, keepdims=True)
        i1 = jnp.min(jnp.where(le == m1, row8, 8), axis=0, keepdims=True)
        le2 = jnp.where(row8 == i1, -jnp.inf, le)
        m2 = jnp.max(le2, axis=0, keepdims=True)
        i2 = jnp.min(jnp.where(le2 == m2, row8, 8), axis=0, keepdims=True)
        tt = jnp.exp(m2 - m1)
        p1 = 1.0 / (1.0 + tt)
        p2 = tt / (1.0 + tt)
        e0 = g_sel * EXPERTS_PER_GROUP + i1
        e1 = g_sel * EXPERTS_PER_GROUP + i2

        rowe = lax.broadcasted_iota(I32, (N_EXPERTS, tm), 0)
        oh0 = (rowe == e0).astype(F32)
        oh1 = (rowe == e1).astype(F32)
        cnt = oh0 + oh1
        ti = lax.broadcasted_iota(I32, (tm, tm), 0)
        tj = lax.broadcasted_iota(I32, (tm, tm), 1)
        upper = jnp.where(ti < tj, 1.0, 0.0).astype(BF16)
        pre = jnp.dot(cnt.astype(BF16), upper, preferred_element_type=F32) + run_ref[:, 0:1]
        rank0 = jnp.sum(oh0 * pre, axis=0, keepdims=True)
        rank1 = jnp.sum(oh1 * pre, axis=0, keepdims=True)
        run_ref[...] = run_ref[...] + jnp.sum(cnt, axis=1, keepdims=True)

        ri_ref[...] = jnp.zeros(ri_ref.shape, I32)
        ri_ref[0:1, :] = e0
        ri_ref[1:2, :] = e1
        ri_ref[2:3, :] = rank0.astype(I32)
        ri_ref[3:4, :] = rank1.astype(I32)
        rw_ref[...] = jnp.zeros(rw_ref.shape, F32)
        rw_ref[0:1, :] = g_w * p1
        rw_ref[1:2, :] = g_w * p2
        cnt_ref[...] = run_ref[...].astype(I32)


def _out_ffn(xp, xs, mixed, w_out, g_ffn, w_rg, b_rg, w_re, b_re):
    tp, ts = xp.shape[0], xs.shape[0]
    t = tp + ts
    tm, tn = TM_OUT, TN_OUT
    assert tp % tm == 0 and ts % tm == 0
    n_p = tp // tm
    nj = D_MODEL // tn
    r0 = ROUTER_EXPERT_ROW0
    wr = jnp.zeros((ROUTER_ROWS, D_MODEL), F32)
    wr = wr.at[0:N_EXPERT_GROUPS].set(w_rg.T).at[r0:r0 + N_EXPERTS].set(w_re.T)
    br = jnp.zeros((ROUTER_ROWS,), F32).at[0:N_EXPERT_GROUPS].set(b_rg).at[r0:r0 + N_EXPERTS].set(b_re)
    br = jnp.broadcast_to(br[:, None], (ROUTER_ROWS, LANE))
    kern = functools.partial(_outffn_kernel, n_p_tiles=n_p, nj=nj, tm=tm, tn=tn)
    return pl.pallas_call(
        kern,
        out_shape=[jax.ShapeDtypeStruct((t * ROW_CHUNKS, LANE), F32),
                   jax.ShapeDtypeStruct((8, t), I32),
                   jax.ShapeDtypeStruct((8, t), F32),
                   jax.ShapeDtypeStruct((N_EXPERTS, LANE), I32)],
        grid=(t // tm, nj),
        in_specs=[
            pl.BlockSpec((tm, tn), lambda i, j: (jnp.minimum(i, n_p - 1), j)),
            pl.BlockSpec((tm, tn), lambda i, j: (jnp.maximum(i - n_p, 0), j)),
            pl.BlockSpec((tm, D_MODEL), lambda i, j: (i, 0)),
            pl.BlockSpec((D_MODEL, tn), lambda i, j: (0, j)),
            pl.BlockSpec((1, D_MODEL), lambda i, j: (0, 0)),
            pl.BlockSpec((ROUTER_ROWS, D_MODEL), lambda i, j: (0, 0)),
            pl.BlockSpec((ROUTER_ROWS, LANE), lambda i, j: (0, 0)),
        ],
        out_specs=[
            pl.BlockSpec((tm * ROW_CHUNKS, LANE), lambda i, j: (i, 0)),
            pl.BlockSpec((8, tm), lambda i, j: (0, i)),
            pl.BlockSpec((8, tm), lambda i, j: (0, i)),
            pl.BlockSpec((N_EXPERTS, LANE), lambda i, j: (0, 0)),
        ],
        scratch_shapes=[pltpu.VMEM((N_EXPERTS, LANE), F32), pltpu.VMEM((tm, D_MODEL), F32)],
        compiler_params=_cparams(("arbitrary", "arbitrary")),
        name="out_ffn",
    )(xp, xs, mixed, w_out.astype(BF16), g_ffn.reshape(1, D_MODEL), wr, br)


def _gather_rows(src_hbm, idx_ref, buf, sem, slot, n):
    def body(r8, c):
        for u in range(DMA_UNROLL):
            r = r8 * DMA_UNROLL + u
            src = pl.multiple_of(idx_ref[0, 0, r] * ROW_CHUNKS, ROW_CHUNKS)
            dst = pl.multiple_of(r * ROW_CHUNKS, ROW_CHUNKS)
            pltpu.make_async_copy(src_hbm.at[pl.ds(src, ROW_CHUNKS)], buf.at[slot, pl.ds(dst, ROW_CHUNKS)],
                                  sem.at[slot]).start()
        return c
    lax.fori_loop(0, n // DMA_UNROLL, body, 0)


def _wait_rows(src_hbm, buf, sem, slot, n):
    pltpu.make_async_copy(src_hbm.at[pl.ds(0, n * ROW_CHUNKS)], buf.at[slot], sem.at[slot]).wait()


def _lane_chunk(c, rows, row0=0):
    return pl.ds(row0 * ROW_CHUNKS + c, rows, stride=ROW_CHUNKS)


def _moe_kernel(blk_e_ref, nused_ref, tok_cur_ref, tok_nxt_ref, x1_hbm, gffn_ref, w1_ref, w3_ref, w2_ref,
                y_ref, xbuf, sem, hbuf, *, blk):
    i = pl.program_id(0)
    nused = nused_ref[0]
    slot = i % 2

    @pl.when(i == 0)
    def _():
        _gather_rows(x1_hbm, tok_cur_ref, xbuf, sem, 0, blk)

    @pl.when(i + 1 < nused)
    def _():
        _gather_rows(x1_hbm, tok_nxt_ref, xbuf, sem, 1 - slot, blk)

    @pl.when(i < nused)
    def _():
        _wait_rows(x1_hbm, xbuf, sem, slot, blk)
        ssq = jnp.zeros((blk, LANE), F32)
        for c in range(ROW_CHUNKS):
            xc = xbuf[slot, _lane_chunk(c, blk), :]
            ssq = ssq + xc * xc
        rs = lax.rsqrt(jnp.sum(ssq, axis=-1, keepdims=True) * (1.0 / D_MODEL) + NORM_EPS)
        for c in range(ROW_CHUNKS):
            sl = slice(c * LANE, (c + 1) * LANE)
            hbuf[:, sl] = (xbuf[slot, _lane_chunk(c, blk), :] * rs * gffn_ref[:, sl]).astype(BF16)
        h = hbuf[...]
        a = jnp.dot(h, w1_ref[0], preferred_element_type=F32)
        b = jnp.dot(h, w3_ref[0], preferred_element_type=F32)
        act = (a * _sigmoid(a) * b).astype(BF16)
        y = jnp.dot(act, w2_ref[0], preferred_element_type=F32)
        for c in range(ROW_CHUNKS):
            y_ref[_lane_chunk(c, blk), :] = y[:, c * LANE:(c + 1) * LANE]

    @pl.when(i >= nused)
    def _():
        y_ref[...] = jnp.zeros(y_ref.shape, F32)


def _moe(x1, g_ffn, w1, w3, w2, blk_e, nused, row_tok, nblk):
    blk = MOE_BLK
    tok3 = row_tok.reshape(nblk, 1, blk)
    grid_spec = pltpu.PrefetchScalarGridSpec(
        num_scalar_prefetch=2,
        grid=(nblk,),
        in_specs=[
            pl.BlockSpec((1, 1, blk), lambda i, be, nu: (i, 0, 0), memory_space=pltpu.SMEM),
            pl.BlockSpec((1, 1, blk), lambda i, be, nu: (jnp.minimum(i + 1, nblk - 1), 0, 0), memory_space=pltpu.SMEM),
            pl.BlockSpec(memory_space=pl.ANY),
            pl.BlockSpec((1, D_MODEL), lambda i, be, nu: (0, 0)),
            pl.BlockSpec((1, D_MODEL, D_EXPERT), lambda i, be, nu: (be[i], 0, 0)),
            pl.BlockSpec((1, D_MODEL, D_EXPERT), lambda i, be, nu: (be[i], 0, 0)),
            pl.BlockSpec((1, D_EXPERT, D_MODEL), lambda i, be, nu: (be[i], 0, 0)),
        ],
        out_specs=pl.BlockSpec((blk * ROW_CHUNKS, LANE), lambda i, be, nu: (i, 0)),
        scratch_shapes=[pltpu.VMEM((2, blk * ROW_CHUNKS, LANE), F32), pltpu.SemaphoreType.DMA((2,)),
                        pltpu.VMEM((blk, D_MODEL), BF16)],
    )
    return pl.pallas_call(
        functools.partial(_moe_kernel, blk=blk),
        out_shape=jax.ShapeDtypeStruct((nblk * blk * ROW_CHUNKS, LANE), F32),
        grid_spec=grid_spec,
        compiler_params=_cparams(("arbitrary",)),
        name="moe",
    )(blk_e, nused, tok3, tok3, x1, g_ffn.reshape(1, D_MODEL), w1.astype(BF16), w3.astype(BF16), w2.astype(BF16))


def _combine_kernel(d_cur_ref, d_nxt_ref, y_hbm, x1_ref, w0_ref, w1_ref, op_ref, os_ref, ybuf, sem,
                    *, tm, n_p_tiles, nt):
    i = pl.program_id(0)
    slot = i % 2

    @pl.when(i == 0)
    def _():
        _gather_rows(y_hbm, d_cur_ref, ybuf, sem, 0, 2 * tm)

    @pl.when(i + 1 < nt)
    def _():
        _gather_rows(y_hbm, d_nxt_ref, ybuf, sem, 1 - slot, 2 * tm)

    _wait_rows(y_hbm, ybuf, sem, slot, 2 * tm)
    w0 = w0_ref[...]
    w1 = w1_ref[...]

    def write(o_ref):
        for c in range(ROW_CHUNKS):
            o_ref[:, c * LANE:(c + 1) * LANE] = (x1_ref[_lane_chunk(c, tm), :]
                                                 + (w0 * ybuf[slot, _lane_chunk(c, tm), :]
                                                    + w1 * ybuf[slot, _lane_chunk(c, tm, tm), :]))

    @pl.when(i < n_p_tiles)
    def _():
        write(op_ref)

    @pl.when(i >= n_p_tiles)
    def _():
        write(os_ref)


def _combine(ymoe, x1, dest0, dest1, w0, w1, tp, ts):
    t = tp + ts
    tm = TM_COMB
    assert tp % tm == 0 and ts % tm == 0
    nt = t // tm
    n_p = tp // tm
    dd = jnp.concatenate([dest0.reshape(nt, 1, tm), dest1.reshape(nt, 1, tm)], axis=2)
    w0c = jnp.broadcast_to(w0[:, None], (t, LANE))
    w1c = jnp.broadcast_to(w1[:, None], (t, LANE))
    kern = functools.partial(_combine_kernel, tm=tm, n_p_tiles=n_p, nt=nt)
    return pl.pallas_call(
        kern,
        out_shape=[jax.ShapeDtypeStruct((tp, D_MODEL), F32), jax.ShapeDtypeStruct((ts, D_MODEL), F32)],
        grid=(nt,),
        in_specs=[
            pl.BlockSpec((1, 1, 2 * tm), lambda i: (i, 0, 0), memory_space=pltpu.SMEM),
            pl.BlockSpec((1, 1, 2 * tm), lambda i: (jnp.minimum(i + 1, nt - 1), 0, 0), memory_space=pltpu.SMEM),
            pl.BlockSpec(memory_space=pl.ANY),
            pl.BlockSpec((tm * ROW_CHUNKS, LANE), lambda i: (i, 0)),
            pl.BlockSpec((tm, LANE), lambda i: (i, 0)),
            pl.BlockSpec((tm, LANE), lambda i: (i, 0)),
        ],
        out_specs=[
            pl.BlockSpec((tm, D_MODEL), lambda i: (jnp.minimum(i, n_p - 1), 0)),
            pl.BlockSpec((tm, D_MODEL), lambda i: (jnp.maximum(i - n_p, 0), 0)),
        ],
        scratch_shapes=[pltpu.VMEM((2, 2 * tm * ROW_CHUNKS, LANE), F32), pltpu.SemaphoreType.DMA((2,))],
        compiler_params=_cparams(("arbitrary",)),
        name="combine",
    )(dd, dd, ymoe, x1, w0c, w1c)


def _route_plan(ri, counts, n_assign):
    blk = MOE_BLK
    nblk = n_assign // blk + N_EXPERTS
    t = ri.shape[1]
    e0, e1, rank0, rank1 = ri[0], ri[1], ri[2], ri[3]
    cnt = counts[:, 0]
    pcnt = (cnt + blk - 1) // blk * blk
    pend = jnp.cumsum(pcnt)
    pstart = pend - pcnt
    nused = (pend[-1] // blk).astype(I32).reshape(1)
    blk_e = jnp.minimum(jnp.searchsorted(pend, jnp.arange(nblk, dtype=I32) * blk, side="right"),
                        N_EXPERTS - 1).astype(I32)
    oh = lambda e: (e[:, None] == jnp.arange(N_EXPERTS, dtype=I32)[None, :])
    dest0 = jnp.sum(jnp.where(oh(e0), pstart[None, :], 0), axis=1).astype(I32) + rank0
    dest1 = jnp.sum(jnp.where(oh(e1), pstart[None, :], 0), axis=1).astype(I32) + rank1
    tok = jnp.arange(t, dtype=I32)
    row_tok = jnp.zeros((nblk * blk,), I32)
    row_tok = row_tok.at[jnp.concatenate([dest0, dest1])].set(jnp.concatenate([tok, tok]), unique_indices=True)
    return blk_e, nused, row_tok, dest0, dest1, nblk


def _layer(xp, xs, sp_len, ss_len, g_mix, w_in, g_qa, g_ka, g_qb, g_kb, w_oa, w_ob, w_out,
           g_ffn, w_rg, b_rg, w_re, b_re, w1, w3, w2):
    tp, ts = xp.shape[0], xs.shape[0]
    t = tp + ts
    qa, kv, pb0, pb1, pb2, gates = _in_proj(xp, xs, sp_len, ss_len, g_mix, w_in, g_qa, g_ka, g_qb, g_kb)
    att_p = _attn_a(qa, kv, 0, tp // sp_len, sp_len)
    att_s = _attn_a(qa, kv, tp, ts // ss_len, ss_len)
    obs, lses = [], []
    for g, ((window, dil), pb) in enumerate(zip(B_PAIRS, (pb0, pb1, pb2))):
        assert window // (2 * dil) == B_HALF_WINDOW
        o, l = _attn_b(pb, dil, [(tp, sp_len), (ts, ss_len)], g)
        obs.append(o)
        lses.append(l)
    mixed = _mix(att_p, att_s, obs, lses, gates, w_oa, w_ob)
    x1, ri, rw, counts = _out_ffn(xp, xs, mixed, w_out, g_ffn, w_rg, b_rg, w_re, b_re)
    blk_e, nused, row_tok, dest0, dest1, nblk = _route_plan(ri, counts, t * TOP_K)
    ymoe = _moe(x1, g_ffn, w1, w3, w2, blk_e, nused, row_tok, nblk)
    return _combine(ymoe, x1, dest0, dest1, rw[0], rw[1], tp, ts)


def kernel(x_prompt, x_sample, g_mix, w_in, g_qa, g_ka, g_qb, g_kb, w_oa, w_ob, w_out, g_ffn, w_rg, b_rg,
           w_re, b_re, w1, w3, w2):
    assert g_mix.shape[0] == 1
    bp, sp_len, d = x_prompt.shape
    bs, ss_len, _ = x_sample.shape
    yp, ys = _layer(x_prompt.reshape(bp * sp_len, d), x_sample.reshape(bs * ss_len, d), sp_len, ss_len,
                    g_mix[0], w_in[0], g_qa[0], g_ka[0], g_qb[0], g_kb[0], w_oa[0], w_ob[0], w_out[0],
                    g_ffn[0], w_rg[0], b_rg[0], w_re[0], b_re[0], w1[0], w3[0], w2[0])
    return yp.reshape(x_prompt.shape), ys.reshape(x_sample.shape)
```
